```python
import math
import jax
import jax.numpy as jnp
from jax import lax
import numpy as np

D_MODEL = 1024
BATCH = 16
SEQ = 2048
DEPTH = 4

LN_EPS = 1e-5
DN_ALPHA = (2.0 * DEPTH) ** 0.25
DN_BETA = (8.0 * DEPTH) ** -0.25
FFN_RES = 0.5
D_FF = ((8 * D_MODEL // 3 + 127) // 128) * 128
POOL_WIDTH = 3 * D_MODEL // 4
POOL_WINDOWS = (2, 4, 8, 16)
POOL_GROUPS = len(POOL_WINDOWS)
POOL_GDIM = POOL_WIDTH // POOL_GROUPS
SSD_INNER = D_MODEL
SSD_HEADDIM = 64
SSD_HEADS = SSD_INNER // SSD_HEADDIM
SSD_GROUPS = 4
SSD_HPG = SSD_HEADS // SSD_GROUPS
SSD_STATE = 128
SSD_CONV = 4
SSD_CHUNK = 128
SSD_CONV_CH = SSD_INNER + 2 * SSD_GROUPS * SSD_STATE
SSD_EPS = 1e-5
ATTN_CONFIGS = ((128, 1), (512, 4), (2048, 16))
ATTN_HEAD_DIM = 64
ATTN_GROUP_HEADS = 4
ATTN_HEADS = ATTN_GROUP_HEADS * len(ATTN_CONFIGS)
ATTN_WIDTH = ATTN_HEADS * ATTN_HEAD_DIM
ATTN_OUT = ATTN_GROUP_HEADS * ATTN_HEAD_DIM
REL_BUCKETS = 32
REL_MAX_DIST = 2048
N_BRANCH = 3
IN_SIZES = (POOL_WIDTH, SSD_INNER, SSD_CONV_CH, SSD_HEADS, ATTN_WIDTH, ATTN_WIDTH, ATTN_WIDTH, N_BRANCH * D_MODEL)
IN_SPLITS = tuple(sum(IN_SIZES[:i + 1]) for i in range(len(IN_SIZES) - 1))
W_IN_COLS = sum(IN_SIZES)
DT_OFFSET = POOL_WIDTH + SSD_INNER + SSD_CONV_CH

kernel_name = 'hybrid_pool_ssd_dilated_attn_block'


def layer_norm(x, g, b):
    xf = x.astype(jnp.float32)
    mu = xf.mean(-1, keepdims=True)
    var = jnp.square(xf - mu).mean(-1, keepdims=True)
    return ((xf - mu) * lax.rsqrt(var + LN_EPS) * g + b).astype(x.dtype)


def swiglu(x, w13, w2):
    a, g = jnp.split(x @ w13, 2, axis=-1)
    return (jax.nn.silu(a) * g) @ w2


def pool_mixer(u, pool_w, pool_b, pool_scale):
    b, s, _ = u.shape
    uf = u.astype(jnp.float32).reshape(b, s, POOL_GROUPS, POOL_GDIM)
    csum = jnp.pad(jnp.cumsum(uf, axis=1), ((0, 0), (1, 0), (0, 0), (0, 0)))
    hi = jnp.arange(1, s + 1)[:, None]
    lo = jnp.maximum(hi - jnp.array(POOL_WINDOWS)[None, :], 0)
    lower = csum[:, lo, jnp.arange(POOL_GROUPS)[None, :], :]
    mean = (csum[:, 1:] - lower) / (hi - lo).astype(jnp.float32)[None, :, :, None]
    pooled = (mean - uf).astype(u.dtype)
    y = jnp.einsum('bsgc,gcd->bsgd', pooled, pool_w) + pool_b
    return y.reshape(b, s, POOL_WIDTH) * pool_scale


def causal_dwconv(x, w, bias):
    y = lax.conv_general_dilated(x, w[:, None, :], window_strides=(1,), padding=[(SSD_CONV - 1, 0)],
                                 dimension_numbers=('NWC', 'WIO', 'NWC'), feature_group_count=x.shape[-1])
    return y + bias


def segsum(a):
    cs = jnp.cumsum(a, axis=-1)
    diff = cs[..., :, None] - cs[..., None, :]
    n = a.shape[-1]
    mask = jnp.tril(jnp.ones((n, n), dtype=bool))
    return jnp.where(mask, diff, -jnp.inf)


def ssd_scan(xh, dt, a, bm, cm):
    b, s, g, e, p = xh.shape
    n = bm.shape[-1]
    nc, l = s // SSD_CHUNK, SSD_CHUNK
    xdt = (xh * dt[..., None]).reshape(b, nc, l, g, e, p)
    adt = (dt * a).reshape(b, nc, l, g, e).transpose(0, 1, 3, 4, 2)
    bm = bm.reshape(b, nc, l, g, n)
    cm = cm.reshape(b, nc, l, g, n)
    a_cum = jnp.cumsum(adt, axis=-1)
    decay = jnp.exp(segsum(adt))
    cb = jnp.einsum('bclgn,bcsgn->bcgls', cm, bm)
    y_diag = jnp.einsum('bcgls,bcgels,bcsgep->bclgep', cb, decay, xdt)
    decay_states = jnp.exp(a_cum[..., -1:] - a_cum)
    states = jnp.einsum('bclgn,bcgel,bclgep->bcgepn', bm, decay_states, xdt)
    chunk_decay = jnp.exp(a_cum[..., -1])

    def step(h, inp):
        st, dec = inp
        return h * dec[..., None, None] + st, h

    h0 = jnp.zeros((b, g, e, p, n), jnp.float32)
    _, prev = lax.scan(step, h0, (states.transpose(1, 0, 2, 3, 4, 5), chunk_decay.transpose(1, 0, 2, 3)))
    prev = prev.transpose(1, 0, 2, 3, 4, 5)
    y_off = jnp.einsum('bclgn,bcgepn,bcgel->bclgep', cm, prev, jnp.exp(a_cum))
    return (y_diag + y_off).reshape(b, s, g, e, p)


def ssd_mixer(z, xbc, dt_raw, conv_w, conv_b, dt_bias, a_log, d_skip, norm_w):
    b, s, _ = z.shape
    xbc = jax.nn.silu(causal_dwconv(xbc, conv_w, conv_b)).astype(jnp.float32)
    xs, bm, cm = jnp.split(xbc, [SSD_INNER, SSD_INNER + SSD_GROUPS * SSD_STATE], axis=-1)
    dt = jax.nn.softplus(dt_raw.astype(jnp.float32) + dt_bias.astype(jnp.float32))
    a = -jnp.exp(a_log.astype(jnp.float32))
    xh = xs.reshape(b, s, SSD_GROUPS, SSD_HPG, SSD_HEADDIM)
    y = ssd_scan(xh, dt.reshape(b, s, SSD_GROUPS, SSD_HPG), a.reshape(SSD_GROUPS, SSD_HPG),
                 bm.reshape(b, s, SSD_GROUPS, SSD_STATE), cm.reshape(b, s, SSD_GROUPS, SSD_STATE))
    y = y + d_skip.astype(jnp.float32).reshape(SSD_GROUPS, SSD_HPG)[..., None] * xh
    y = y.reshape(b, s, SSD_INNER) * jax.nn.silu(z.astype(jnp.float32))
    yg = y.reshape(b, s, SSD_GROUPS, SSD_INNER // SSD_GROUPS)
    yg = yg * lax.rsqrt(jnp.mean(jnp.square(yg), axis=-1, keepdims=True) + SSD_EPS)
    return (yg.reshape(b, s, SSD_INNER) * norm_w).astype(z.dtype)


def t5_bucket_np(dist):
    dist = np.maximum(dist, 0)
    max_exact = REL_BUCKETS // 2
    large = max_exact + (np.log(np.maximum(dist, 1) / max_exact) / np.log(REL_MAX_DIST / max_exact)
                         * (REL_BUCKETS - max_exact)).astype(np.int32)
    large = np.minimum(large, REL_BUCKETS - 1)
    return np.where(dist < max_exact, dist, large).astype(np.int32)


def dilated_group_attention(q, k, v, bias_tab, window, dilation):
    b, s, h, dh = q.shape
    span = window // dilation
    blk = span
    sub_len = s // dilation
    n_blk = -(-sub_len // blk)
    lp = n_blk * blk

    def to_sub(t):
        t = t.reshape(b, sub_len, dilation, h, dh).transpose(0, 2, 1, 3, 4)
        return jnp.pad(t, ((0, 0), (0, 0), (0, lp - sub_len), (0, 0), (0, 0)))

    def band(t):
        tb = t.reshape(b, dilation, n_blk, blk, h, dh)
        prev = jnp.pad(tb, ((0, 0), (0, 0), (1, 0), (0, 0), (0, 0), (0, 0)))[:, :, :-1]
        return jnp.concatenate([prev, tb], axis=3)

    qb = to_sub(q).reshape(b, dilation, n_blk, blk, h, dh)
    kb = band(to_sub(k))
    vb = band(to_sub(v))
    qi = np.arange(blk)[:, None]
    kj = np.arange(2 * blk)[None, :]
    delta = qi - kj + blk
    in_band = (delta >= 0) & (delta <= span)
    has_prev = (np.arange(n_blk)[:, None, None] > 0) | (kj >= blk)[None]
    valid = jnp.asarray(in_band[None] & has_prev)
    bias = jnp.take(bias_tab, jnp.asarray(t5_bucket_np(delta * dilation)), axis=0)
    bias = bias.astype(jnp.float32).transpose(2, 0, 1)
    logits = jnp.einsum('brnqhd,brnkhd->brnhqk', qb, kb).astype(jnp.float32) * (dh ** -0.5) + bias
    logits = jnp.where(valid[None, None, :, None], logits, -jnp.inf)
    m = jnp.max(logits, axis=-1, keepdims=True)
    pexp = jnp.exp(logits - m)
    den = jnp.sum(pexp, axis=-1, keepdims=True)
    out = jnp.einsum('brnhqk,brnkhd->brnqhd', (pexp / den).astype(v.dtype), vb)
    lse = (m + jnp.log(den))[..., 0]
    out = out.reshape(b, dilation, lp, h, dh)[:, :, :sub_len].transpose(0, 2, 1, 3, 4).reshape(b, s, h, dh)
    lse = lse.transpose(0, 1, 2, 4, 3).reshape(b, dilation, lp, h)[:, :, :sub_len]
    lse = lse.transpose(0, 2, 1, 3).reshape(b, s, h)
    return out, lse


def dilated_attention(q, k, v, rel_bias):
    b, s = q.shape[:2]
    outs, lses = [], []
    for gi, (window, dilation) in enumerate(ATTN_CONFIGS):
        sl = slice(gi * ATTN_GROUP_HEADS, (gi + 1) * ATTN_GROUP_HEADS)
        o, l = dilated_group_attention(q[:, :, sl], k[:, :, sl], v[:, :, sl], rel_bias[:, sl], window, dilation)
        outs.append(o)
        lses.append(l)
    wgt = jax.nn.softmax(jnp.stack(lses, axis=0), axis=0)
    y = jnp.einsum('gbsh,gbshd->bshd', wgt, jnp.stack(outs, axis=0).astype(jnp.float32))
    return y.reshape(b, s, ATTN_OUT).astype(q.dtype)


def hybrid_mixer(x, w_in, gate_b, pool_w, pool_b, pool_scale, conv_w, conv_b, dt_bias, a_log, d_skip,
                 ssd_norm, rel_bias, p_pool, p_ssd, p_attn, w_out):
    b, s, _ = x.shape
    hcat = x @ w_in
    u_pool, z, xbc, dt_raw, q, k, v, gates = jnp.split(hcat, IN_SPLITS, axis=-1)
    y_a = pool_mixer(u_pool, pool_w, pool_b, pool_scale) @ p_pool
    y_b = ssd_mixer(z, xbc, dt_raw, conv_w, conv_b, dt_bias, a_log, d_skip, ssd_norm) @ p_ssd
    hs = (b, s, ATTN_HEADS, ATTN_HEAD_DIM)
    y_c = dilated_attention(q.reshape(hs), k.reshape(hs), v.reshape(hs), rel_bias) @ p_attn
    g = jax.nn.sigmoid(gates.reshape(b, s, N_BRANCH, D_MODEL) + gate_b)
    merged = g[:, :, 0] * y_a + g[:, :, 1] * y_b + g[:, :, 2] * y_c
    return merged @ w_out


def setup_inputs(seed: int = 0) -> dict:
    key = jax.random.key(seed)
    ks = jax.random.split(key, 32)
    f32 = jnp.float32

    def nrm(k, shape, scale):
        return jax.random.normal(k, shape, f32) * scale

    x = jax.random.normal(ks[0], (BATCH, SEQ, D_MODEL), f32)
    col_scale = jnp.ones((W_IN_COLS,), f32).at[DT_OFFSET:DT_OFFSET + SSD_HEADS].set(0.1)
    w_in = nrm(ks[1], (DEPTH, D_MODEL, W_IN_COLS), D_MODEL ** -0.5) * col_scale
    dt0 = jnp.exp(jax.random.uniform(ks[2], (DEPTH, SSD_HEADS), f32, math.log(1e-3), math.log(1e-1)))
    return {
        'x': x,
        'ffn1_w13': nrm(ks[3], (DEPTH, D_MODEL, 2 * D_FF), D_MODEL ** -0.5),
        'ffn1_w2': nrm(ks[4], (DEPTH, D_FF, D_MODEL), D_FF ** -0.5 * DN_BETA),
        'ln1_g': 1.0 + nrm(ks[5], (DEPTH, D_MODEL), 0.02),
        'ln1_b': nrm(ks[6], (DEPTH, D_MODEL), 0.02),
        'w_in': w_in,
        'gate_b': nrm(ks[7], (DEPTH, N_BRANCH, D_MODEL), 0.1),
        'pool_w': nrm(ks[8], (DEPTH, POOL_GROUPS, POOL_GDIM, POOL_GDIM), POOL_GDIM ** -0.5),
        'pool_b': nrm(ks[9], (DEPTH, POOL_GROUPS, POOL_GDIM), 0.02),
        'pool_scale': 1.0 + nrm(ks[10], (DEPTH, POOL_WIDTH), 0.02),
        'conv_w': nrm(ks[11], (DEPTH, SSD_CONV, SSD_CONV_CH), SSD_CONV ** -0.5),
        'conv_b': nrm(ks[12], (DEPTH, SSD_CONV_CH), 0.02),
        'dt_bias': dt0 + jnp.log(-jnp.expm1(-dt0)),
        'a_log': jnp.log(jax.random.uniform(ks[13], (DEPTH, SSD_HEADS), f32, 1.0, 16.0)),
        'd_skip': 1.0 + nrm(ks[14], (DEPTH, SSD_HEADS), 0.02),
        'ssd_norm': 1.0 + nrm(ks[15], (DEPTH, SSD_INNER), 0.02),
        'rel_bias': nrm(ks[16], (REL_BUCKETS, ATTN_HEADS), 0.2),
        'p_pool': nrm(ks[17], (DEPTH, POOL_WIDTH, D_MODEL), POOL_WIDTH ** -0.5),
        'p_ssd': nrm(ks[18], (DEPTH, SSD_INNER, D_MODEL), SSD_INNER ** -0.5),
        'p_attn': nrm(ks[19], (DEPTH, ATTN_OUT, D_MODEL), ATTN_OUT ** -0.5),
        'w_out': nrm(ks[20], (DEPTH, D_MODEL, D_MODEL), D_MODEL ** -0.5 * DN_BETA),
        'ln2_g': 1.0 + nrm(ks[21], (DEPTH, D_MODEL), 0.02),
        'ln2_b': nrm(ks[22], (DEPTH, D_MODEL), 0.02),
        'ffn2_w13': nrm(ks[23], (DEPTH, D_MODEL, 2 * D_FF), D_MODEL ** -0.5),
        'ffn2_w2': nrm(ks[24], (DEPTH, D_FF, D_MODEL), D_FF ** -0.5 * DN_BETA),
        'ln3_g': 1.0 + nrm(ks[25], (DEPTH, D_MODEL), 0.02),
        'ln3_b': nrm(ks[26], (DEPTH, D_MODEL), 0.02),
    }


def reference(x, ffn1_w13, ffn1_w2, ln1_g, ln1_b, w_in, gate_b, pool_w, pool_b, pool_scale, conv_w, conv_b,
              dt_bias, a_log, d_skip, ssd_norm, rel_bias, p_pool, p_ssd, p_attn, w_out, ln2_g, ln2_b,
              ffn2_w13, ffn2_w2, ln3_g, ln3_b):
    for i in range(DEPTH):
        x = layer_norm(DN_ALPHA * x + FFN_RES * swiglu(x, ffn1_w13[i], ffn1_w2[i]), ln1_g[i], ln1_b[i])
        mix = hybrid_mixer(x, w_in[i], gate_b[i], pool_w[i], pool_b[i], pool_scale[i], conv_w[i], conv_b[i],
                           dt_bias[i], a_log[i], d_skip[i], ssd_norm[i], rel_bias, p_pool[i], p_ssd[i],
                           p_attn[i], w_out[i])
        x = layer_norm(DN_ALPHA * x + mix, ln2_g[i], ln2_b[i])
        x = layer_norm(DN_ALPHA * x + FFN_RES * swiglu(x, ffn2_w13[i], ffn2_w2[i]), ln3_g[i], ln3_b[i])
    return x
```

```python
import functools

import numpy as np
import jax
import jax.numpy as jnp
from jax import lax
from jax.experimental import pallas as pl
from jax.experimental.pallas import tpu as pltpu

F32 = jnp.float32
BF16 = jnp.bfloat16

D_MODEL = 1024
BATCH = 16
SEQ = 2048
DEPTH = 4
N_TOK = BATCH * SEQ
LN_EPS = 1e-5
DN_ALPHA = (2.0 * DEPTH) ** 0.25
FFN_RES = 0.5
D_FF = ((8 * D_MODEL // 3 + 127) // 128) * 128
POOL_WIDTH = 3 * D_MODEL // 4
POOL_WINDOWS = (2, 4, 8, 16)
POOL_GROUPS = len(POOL_WINDOWS)
POOL_GDIM = POOL_WIDTH // POOL_GROUPS
SSD_INNER = D_MODEL
SSD_HEADDIM = 64
SSD_HEADS = SSD_INNER // SSD_HEADDIM
SSD_GROUPS = 4
SSD_HPG = SSD_HEADS // SSD_GROUPS
SSD_STATE = 128
SSD_CONV = 4
SSD_CHUNK = 128
SSD_CONV_CH = SSD_INNER + 2 * SSD_GROUPS * SSD_STATE
SSD_EPS = 1e-5
ATTN_CONFIGS = ((128, 1), (512, 4), (2048, 16))
ATTN_HEAD_DIM = 64
ATTN_GROUP_HEADS = 4
ATTN_HEADS = ATTN_GROUP_HEADS * len(ATTN_CONFIGS)
ATTN_WIDTH = ATTN_HEADS * ATTN_HEAD_DIM
ATTN_OUT = ATTN_GROUP_HEADS * ATTN_HEAD_DIM
ATTN_SPAN = 128
REL_BUCKETS = 32
REL_MAX_DIST = 2048
N_BRANCH = 3
IN_SIZES = (POOL_WIDTH, SSD_INNER, SSD_CONV_CH, SSD_HEADS, ATTN_WIDTH, ATTN_WIDTH, ATTN_WIDTH, N_BRANCH * D_MODEL)
IN_OFFS = tuple(sum(IN_SIZES[:i]) for i in range(len(IN_SIZES) + 1))

LANES = 128
MXU_DIM = 256
VMEM_LIMIT = 56 * 1024 * 1024

TM = 512
FFN_CHUNK = MXU_DIM
POOL_GPAD = MXU_DIM
POOL_PAD = POOL_GROUPS * POOL_GPAD
POOL_HALO = 16
POOL_ROWS = 256
DT_PAD = LANES
CONV_HALO = 8
SSD_TS = 512
NEG_BIG = -1e30

assert all(w // d == ATTN_SPAN for w, d in ATTN_CONFIGS)
assert POOL_HALO >= max(POOL_WINDOWS) and CONV_HALO >= SSD_CONV - 1

_NT = (((1,), (1,)), ((), ()))
_TN = (((0,), (0,)), ((), ()))


def _dot(a, b):
    return jnp.dot(a, b, preferred_element_type=F32)


def _silu(a):
    return a / (1.0 + jnp.exp(-a))


def _sigmoid(a):
    return 1.0 / (1.0 + jnp.exp(-a))


def _layer_norm(r, g, b):
    mu = jnp.mean(r, axis=-1, keepdims=True)
    c = r - mu
    var = jnp.mean(c * c, axis=-1, keepdims=True)
    return c * lax.rsqrt(var + LN_EPS) * g + b


def _layer_spec(block, layer):
    nd = len(block)
    return pl.BlockSpec((None,) + tuple(block), lambda *_: (layer,) + (0,) * nd,
                        pipeline_mode=pl.Buffered(1))


def _params(*sem):
    return pltpu.CompilerParams(dimension_semantics=sem, vmem_limit_bytes=VMEM_LIMIT)


def _ffn_body(x_ref, w13_ref, w2_ref, g_ref, b_ref, *rest, emit_bf16):
    if emit_bf16:
        o_ref, ob_ref, h_ref = rest
    else:
        o_ref, h_ref = rest
    x = x_ref[...]
    xb = x.astype(BF16)
    for c in range(D_FF // FFN_CHUNK):
        lo = c * FFN_CHUNK
        a = _dot(xb, w13_ref[:, lo:lo + FFN_CHUNK])
        g = _dot(xb, w13_ref[:, D_FF + lo:D_FF + lo + FFN_CHUNK])
        h_ref[:, lo:lo + FFN_CHUNK] = (_silu(a) * g).astype(BF16)
    y = _dot(h_ref[...], w2_ref[...])
    out = _layer_norm(DN_ALPHA * x + FFN_RES * y, g_ref[...], b_ref[...])
    o_ref[...] = out
    if emit_bf16:
        ob_ref[...] = out.astype(BF16)


def _ffn(x, w13, w2, g, b, layer, emit_bf16):
    row = pl.BlockSpec((TM, D_MODEL), lambda t: (t, 0))
    out_shape = [jax.ShapeDtypeStruct((N_TOK, D_MODEL), F32)]
    out_specs = [row]
    if emit_bf16:
        out_shape.append(jax.ShapeDtypeStruct((N_TOK, D_MODEL), BF16))
        out_specs.append(row)
    return pl.pallas_call(
        functools.partial(_ffn_body, emit_bf16=emit_bf16),
        grid=(N_TOK // TM,),
        in_specs=[row,
                  _layer_spec((D_MODEL, 2 * D_FF), layer),
                  _layer_spec((D_FF, D_MODEL), layer),
                  _layer_spec((1, D_MODEL), layer),
                  _layer_spec((1, D_MODEL), layer)],
        out_specs=out_specs,
        out_shape=out_shape,
        scratch_shapes=[pltpu.VMEM((TM, D_FF), BF16)],
        compiler_params=_params("parallel"),
        name="ffn",
    )(x, w13, w2, g, b)


def _pool_body(xb_ref, w_ref, pw_ref, pb_ref, ps_ref, o_ref, u_ref):
    u_ref[0:POOL_HALO, :] = jnp.zeros((POOL_HALO, POOL_PAD), F32)

    def proj(j, carry):
        r0 = pl.multiple_of(j * TM, TM)
        u_ref[pl.ds(POOL_HALO + r0, TM), :] = _dot(xb_ref[pl.ds(r0, TM), :], w_ref[...])
        return carry

    lax.fori_loop(0, SEQ // TM, proj, 0)

    def chunk(c, carry):
        r0 = pl.multiple_of(c * POOL_ROWS, POOL_ROWS)
        t = r0 + lax.broadcasted_iota(jnp.int32, (POOL_ROWS, 1), 0)
        for g, win in enumerate(POOL_WINDOWS):
            cols = slice(g * POOL_GPAD, (g + 1) * POOL_GPAD)
            ug = u_ref[pl.ds(r0, POOL_ROWS + POOL_HALO), cols]
            s = ug + pltpu.roll(ug, 1, 0)
            k = 2
            while k < win:
                s = s + pltpu.roll(s, k, 0)
                k *= 2
            inv = 1.0 / jnp.minimum(t + 1, win).astype(F32)
            pooled = s[POOL_HALO:] * inv - ug[POOL_HALO:]
            y = _dot(pooled.astype(BF16), pw_ref[g])
            y = (y + pb_ref[g:g + 1, :]) * ps_ref[g:g + 1, :]
            o_ref[pl.ds(r0, POOL_ROWS), cols] = y.astype(BF16)
        return carry

    lax.fori_loop(0, SEQ // POOL_ROWS, chunk, 0)


def _pool(xb, w, pw, pb, ps, layer):
    seq = pl.BlockSpec((SEQ, D_MODEL), lambda b: (b, 0))
    return pl.pallas_call(
        _pool_body,
        grid=(BATCH,),
        in_specs=[seq,
                  _layer_spec((D_MODEL, POOL_PAD), layer),
                  _layer_spec((POOL_GROUPS, POOL_GPAD, POOL_GPAD), layer),
                  _layer_spec((POOL_GROUPS, POOL_GPAD), layer),
                  _layer_spec((POOL_GROUPS, POOL_GPAD), layer)],
        out_specs=pl.BlockSpec((SEQ, POOL_PAD), lambda b: (b, 0)),
        out_shape=jax.ShapeDtypeStruct((N_TOK, POOL_PAD), BF16),
        scratch_shapes=[pltpu.VMEM((POOL_HALO + SEQ, POOL_PAD), F32)],
        compiler_params=_params("parallel"),
        name="pool_mixer",
    )(xb, w, pw, pb, ps)


def _split3_bf16(v):
    hi = v.astype(BF16)
    r1 = v - hi.astype(F32)
    mid = r1.astype(BF16)
    lo = (r1 - mid.astype(F32)).astype(BF16)
    return hi, mid, lo


def _ssd_body(xb_ref, wz_ref, wx_ref, wdt_ref, cw_ref, cb_ref, dtb_ref, alog_ref, dsk_ref, nw_ref, o_ref,
              pad_ref, xc_ref, dt_ref, adt_ref, y_ref, st_ref):
    @pl.when(pl.program_id(1) == 0)
    def _():
        st_ref[...] = jnp.zeros(st_ref.shape, F32)
        pad_ref[0:CONV_HALO, :] = jnp.zeros((CONV_HALO, SSD_CONV_CH), F32)

    xb = xb_ref[...]
    pad_ref[CONV_HALO:CONV_HALO + SSD_TS, :] = _dot(xb, wx_ref[...])
    acc = cb_ref[...]
    for j in range(SSD_CONV):
        off = CONV_HALO - (SSD_CONV - 1) + j
        acc = acc + cw_ref[j:j + 1, :] * pad_ref[pl.ds(off, SSD_TS), :]
    xc_ref[...] = _silu(acc)
    pad_ref[0:CONV_HALO, :] = pad_ref[SSD_TS:SSD_TS + CONV_HALO, :]

    dtr = _dot(xb, wdt_ref[...]) + dtb_ref[...]
    dt = jnp.maximum(dtr, 0.0) + jnp.log1p(jnp.exp(-jnp.abs(dtr)))
    dt_ref[...] = dt
    adt_ref[...] = dt * (-jnp.exp(alog_ref[...]))

    li = lax.broadcasted_iota(jnp.int32, (SSD_CHUNK, SSD_CHUNK), 0)
    si = lax.broadcasted_iota(jnp.int32, (SSD_CHUNK, SSD_CHUNK), 1)
    tril = li >= si
    ones_tril = jnp.where(tril, 1.0, 0.0).astype(BF16)

    def chunk(k, carry):
        r0 = pl.multiple_of(k * SSD_CHUNK, SSD_CHUNK)
        rows = pl.ds(r0, SSD_CHUNK)
        hi, mid, lo = _split3_bf16(adt_ref[rows, :])
        acum = _dot(ones_tril, hi) + _dot(ones_tril, mid) + _dot(ones_tril, lo)
        acum_t = acum.T
        dtc = dt_ref[rows, :]
        a_last = acum[SSD_CHUNK - 1:SSD_CHUNK, :]
        e_cum = jnp.exp(acum)
        e_tail = jnp.exp(a_last - acum)
        e_last = jnp.exp(a_last)
        for g in range(SSD_GROUPS):
            b_lo = SSD_INNER + g * SSD_STATE
            c_lo = SSD_INNER + SSD_GROUPS * SSD_STATE + g * SSD_STATE
            bmb = xc_ref[rows, b_lo:b_lo + SSD_STATE].astype(BF16)
            cmb = xc_ref[rows, c_lo:c_lo + SSD_STATE].astype(BF16)
            cbm = lax.dot_general(cmb, bmb, _NT, preferred_element_type=F32)
            for e in range(SSD_HPG):
                h = g * SSD_HPG + e
                hc = slice(h, h + 1)
                pc = slice(h * SSD_HEADDIM, (h + 1) * SSD_HEADDIM)
                xdt = xc_ref[rows, pc] * dtc[:, hc]
                diff = acum[:, hc] - acum_t[hc, :]
                decay = jnp.exp(jnp.where(tril, diff, -jnp.inf))
                y_diag = _dot((cbm * decay).astype(BF16), xdt.astype(BF16))
                prev = st_ref[h]
                y_off = _dot(cmb, prev.astype(BF16)) * e_cum[:, hc]
                y_ref[rows, pc] = y_diag + y_off
                new = lax.dot_general(bmb, (xdt * e_tail[:, hc]).astype(BF16), _TN,
                                      preferred_element_type=F32)
                st_ref[h] = prev * e_last[:, hc] + new
        return carry

    lax.fori_loop(0, SSD_TS // SSD_CHUNK, chunk, 0)

    y = y_ref[...] + dsk_ref[...] * xc_ref[:, 0:SSD_INNER]
    y = y * _silu(_dot(xb, wz_ref[...]))
    gw = SSD_INNER // SSD_GROUPS
    for g in range(SSD_GROUPS):
        cols = slice(g * gw, (g + 1) * gw)
        yg = y[:, cols]
        ms = jnp.mean(yg * yg, axis=-1, keepdims=True)
        o_ref[:, cols] = (yg * lax.rsqrt(ms + SSD_EPS) * nw_ref[:, cols]).astype(BF16)


def _ssd(xb, wz, wx, wdt, cw, cb, dtb, alog, dsk, nw, layer):
    steps = SEQ // SSD_TS
    row = pl.BlockSpec((SSD_TS, D_MODEL), lambda b, s: (b * steps + s, 0))
    return pl.pallas_call(
        _ssd_body,
        grid=(BATCH, steps),
        in_specs=[row,
                  _layer_spec((D_MODEL, SSD_INNER), layer),
                  _layer_spec((D_MODEL, SSD_CONV_CH), layer),
                  _layer_spec((D_MODEL, DT_PAD), layer),
                  _layer_spec((SSD_CONV, SSD_CONV_CH), layer),
                  _layer_spec((1, SSD_CONV_CH), layer),
                  _layer_spec((1, DT_PAD), layer),
                  _layer_spec((1, DT_PAD), layer),
                  _layer_spec((1, SSD_INNER), layer),
                  _layer_spec((1, SSD_INNER), layer)],
        out_specs=pl.BlockSpec((SSD_TS, SSD_INNER), lambda b, s: (b * steps + s, 0)),
        out_shape=jax.ShapeDtypeStruct((N_TOK, SSD_INNER), BF16),
        scratch_shapes=[pltpu.VMEM((CONV_HALO + SSD_TS, SSD_CONV_CH), F32),
                        pltpu.VMEM((SSD_TS, SSD_CONV_CH), F32),
                        pltpu.VMEM((SSD_TS, DT_PAD), F32),
                        pltpu.VMEM((SSD_TS, DT_PAD), F32),
                        pltpu.VMEM((SSD_TS, SSD_INNER), F32),
                        pltpu.VMEM((SSD_HEADS, SSD_STATE, SSD_HEADDIM), F32)],
        compiler_params=_params("arbitrary", "arbitrary"),
        name="ssd_mixer",
    )(xb, wz, wx, wdt, cw, cb, dtb, alog, dsk, nw)


def _attn_body(xb_ref, w_ref, bias_ref, o_ref, tile_ref, qp_ref, kp_ref, vp_ref, op_ref, lp_ref, oa_ref, la_ref):
    n_lt = ATTN_OUT // LANES
    lane = lax.broadcasted_iota(jnp.int32, (ATTN_SPAN, LANES), 1)
    first_head = lane < ATTN_HEAD_DIM
    for t in range(n_lt):
        kp_ref[t, 0:ATTN_SPAN, :] = jnp.zeros((ATTN_SPAN, LANES), F32)
        vp_ref[t, 0:ATTN_SPAN, :] = jnp.zeros((ATTN_SPAN, LANES), F32)

    for g, (_, dil) in enumerate(ATTN_CONFIGS):
        sub = SEQ // dil
        n_blk = sub // ATTN_SPAN
        per = TM // dil
        wcol = g * 3 * ATTN_OUT

        for j in range(SEQ // TM):
            qkv = _dot(xb_ref[j * TM:(j + 1) * TM, :], w_ref[:, wcol:wcol + 3 * ATTN_OUT])
            for which, dst, lead in ((0, qp_ref, 0), (1, kp_ref, ATTN_SPAN), (2, vp_ref, ATTN_SPAN)):
                for t in range(n_lt):
                    c0 = which * ATTN_OUT + t * LANES
                    piece = qkv[:, c0:c0 + LANES]
                    if which == 0:
                        piece = piece * (ATTN_HEAD_DIM ** -0.5)
                    if dil == 1:
                        dst[t, lead + j * TM:lead + (j + 1) * TM, :] = piece
                    else:
                        tile_ref[which * n_lt + t] = piece
                        for r in range(dil):
                            dst[t, pl.ds(lead + r * sub + j * per, per), :] = (
                                tile_ref[which * n_lt + t, pl.ds(r, per, stride=dil), :])

        def block(u, carry, g=g, n_blk=n_blk, dil=dil):
            r0 = pl.multiple_of(u * ATTN_SPAN, ATTN_SPAN)
            rows = pl.ds(r0, ATTN_SPAN)
            prev_rows = pl.ds(r0, ATTN_SPAN)
            cur_rows = pl.ds(r0 + ATTN_SPAN, ATTN_SPAN)
            prev_mask = jnp.where(u % n_blk == 0, NEG_BIG, 0.0).astype(F32)
            for t in range(n_lt):
                q = qp_ref[t, rows, :]
                kprev = kp_ref[t, prev_rows, :].astype(BF16)
                kcur = kp_ref[t, cur_rows, :].astype(BF16)
                vprev = vp_ref[t, prev_rows, :].astype(BF16)
                vcur = vp_ref[t, cur_rows, :].astype(BF16)
                outs, lses = [], []
                for e in range(2):
                    h = g * ATTN_GROUP_HEADS + t * 2 + e
                    mine = first_head if e == 0 else jnp.logical_not(first_head)
                    qh = jnp.where(mine, q, 0.0).astype(BF16)
                    s_prev = (lax.dot_general(qh, kprev, _NT, preferred_element_type=F32)
                              + bias_ref[h, :, 0:ATTN_SPAN] + prev_mask)
                    s_cur = (lax.dot_general(qh, kcur, _NT, preferred_element_type=F32)
                             + bias_ref[h, :, ATTN_SPAN:2 * ATTN_SPAN])
                    m = jnp.maximum(jnp.max(s_prev, axis=-1, keepdims=True),
                                    jnp.max(s_cur, axis=-1, keepdims=True))
                    p_prev = jnp.exp(s_prev - m)
                    p_cur = jnp.exp(s_cur - m)
                    den = jnp.sum(p_prev, axis=-1, keepdims=True) + jnp.sum(p_cur, axis=-1, keepdims=True)
                    pv = _dot(p_prev.astype(BF16), vprev) + _dot(p_cur.astype(BF16), vcur)
                    outs.append(pv / den)
                    lses.append(jnp.broadcast_to(m + jnp.log(den), (ATTN_SPAN, LANES)))
                o_pair = jnp.where(first_head, outs[0], outs[1])
                l_pair = jnp.where(first_head, lses[0], lses[1])
                if dil == 1:
                    oa_ref[t, rows, :] = o_pair
                    la_ref[t, rows, :] = l_pair
                else:
                    op_ref[t, rows, :] = o_pair
                    lp_ref[t, rows, :] = l_pair
            return carry

        lax.fori_loop(0, SEQ // ATTN_SPAN, block, 0)

        if dil > 1:
            for t in range(n_lt):
                for r in range(dil):
                    nat = pl.ds(r, sub, stride=dil)
                    o_g = op_ref[t, r * sub:(r + 1) * sub, :]
                    l_g = lp_ref[t, r * sub:(r + 1) * sub, :]
                    o_a = oa_ref[t, nat, :]
                    l_a = la_ref[t, nat, :]
                    mx = jnp.maximum(l_a, l_g)
                    l_n = mx + jnp.log(jnp.exp(l_a - mx) + jnp.exp(l_g - mx))
                    oa_ref[t, nat, :] = o_a * jnp.exp(l_a - l_n) + o_g * jnp.exp(l_g - l_n)
                    la_ref[t, nat, :] = l_n

    for t in range(n_lt):
        o_ref[:, t * LANES:(t + 1) * LANES] = oa_ref[t].astype(BF16)


def _attn(xb, w, bias, layer):
    n_lt = ATTN_OUT // LANES
    return pl.pallas_call(
        _attn_body,
        grid=(BATCH,),
        in_specs=[pl.BlockSpec((SEQ, D_MODEL), lambda b: (b, 0)),
                  _layer_spec((D_MODEL, 3 * ATTN_WIDTH), layer),
                  pl.BlockSpec((ATTN_HEADS, ATTN_SPAN, 2 * ATTN_SPAN), lambda b: (0, 0, 0),
                               pipeline_mode=pl.Buffered(1))],
        out_specs=pl.BlockSpec((SEQ, ATTN_OUT), lambda b: (b, 0)),
        out_shape=jax.ShapeDtypeStruct((N_TOK, ATTN_OUT), BF16),
        scratch_shapes=[pltpu.VMEM((3 * n_lt, TM, LANES), F32),
                        pltpu.VMEM((n_lt, SEQ, LANES), F32),
                        pltpu.VMEM((n_lt, ATTN_SPAN + SEQ, LANES), F32),
                        pltpu.VMEM((n_lt, ATTN_SPAN + SEQ, LANES), F32),
                        pltpu.VMEM((n_lt, SEQ, LANES), F32),
                        pltpu.VMEM((n_lt, SEQ, LANES), F32),
                        pltpu.VMEM((n_lt, SEQ, LANES), F32),
                        pltpu.VMEM((n_lt, SEQ, LANES), F32)],
        compiler_params=_params("parallel"),
        name="dilated_attn",
    )(xb, w, bias)


def _merge_body(x_ref, ya_ref, yb_ref, yc_ref, wg_ref, gb_ref, pa_ref, pb_ref, pc_ref, wo_ref, g_ref, b_ref, o_ref):
    x = x_ref[...]
    xb = x.astype(BF16)
    merged = None
    for i, (y_ref, p_ref) in enumerate(((ya_ref, pa_ref), (yb_ref, pb_ref), (yc_ref, pc_ref))):
        gate = _sigmoid(_dot(xb, wg_ref[:, i * D_MODEL:(i + 1) * D_MODEL]) + gb_ref[i:i + 1, :])
        term = gate * _dot(y_ref[...], p_ref[...])
        merged = term if merged is None else merged + term
    mix = _dot(merged.astype(BF16), wo_ref[...])
    o_ref[...] = _layer_norm(DN_ALPHA * x + mix, g_ref[...], b_ref[...])


def _merge(x, ya, yb, yc, wg, gb, pa, pb, pc, wo, g, b, layer):
    def row(width):
        return pl.BlockSpec((TM, width), lambda t: (t, 0))

    return pl.pallas_call(
        _merge_body,
        grid=(N_TOK // TM,),
        in_specs=[row(D_MODEL), row(POOL_PAD), row(SSD_INNER), row(ATTN_OUT),
                  _layer_spec((D_MODEL, N_BRANCH * D_MODEL), layer),
                  _layer_spec((N_BRANCH, D_MODEL), layer),
                  _layer_spec((POOL_PAD, D_MODEL), layer),
                  _layer_spec((SSD_INNER, D_MODEL), layer),
                  _layer_spec((ATTN_OUT, D_MODEL), layer),
                  _layer_spec((D_MODEL, D_MODEL), layer),
                  _layer_spec((1, D_MODEL), layer),
                  _layer_spec((1, D_MODEL), layer)],
        out_specs=row(D_MODEL),
        out_shape=jax.ShapeDtypeStruct((N_TOK, D_MODEL), F32),
        compiler_params=_params("parallel"),
        name="gated_merge",
    )(x, ya, yb, yc, wg, gb, pa, pb, pc, wo, g, b)


def _t5_bucket(dist):
    dist = np.maximum(dist, 0)
    max_exact = REL_BUCKETS // 2
    large = max_exact + (np.log(np.maximum(dist, 1) / max_exact) / np.log(REL_MAX_DIST / max_exact)
                         * (REL_BUCKETS - max_exact)).astype(np.int32)
    large = np.minimum(large, REL_BUCKETS - 1)
    return np.where(dist < max_exact, dist, large).astype(np.int32)


def _attn_bias(rel_bias):
    qi = np.arange(ATTN_SPAN)[:, None]
    kj = np.arange(2 * ATTN_SPAN)[None, :]
    delta = qi - kj + ATTN_SPAN
    in_band = jnp.asarray((delta >= 0) & (delta <= ATTN_SPAN))
    mats = []
    for gi, (_, dil) in enumerate(ATTN_CONFIGS):
        tab = rel_bias[:, gi * ATTN_GROUP_HEADS:(gi + 1) * ATTN_GROUP_HEADS].astype(F32)
        b = jnp.take(tab, jnp.asarray(_t5_bucket(delta * dil)), axis=0)
        b = jnp.where(in_band[..., None], b, NEG_BIG)
        mats.append(b.transpose(2, 0, 1))
    return jnp.concatenate(mats, axis=0)


def _pad_groups(a, axis):
    shp = a.shape
    a = a.reshape(shp[:axis] + (POOL_GROUPS, POOL_GDIM) + shp[axis + 1:])
    pad = [(0, 0)] * a.ndim
    pad[axis + 1] = (0, POOL_GPAD - POOL_GDIM)
    a = jnp.pad(a, pad)
    return a.reshape(shp[:axis] + (POOL_PAD,) + shp[axis + 1:])


def kernel(x, ffn1_w13, ffn1_w2, ln1_g, ln1_b, w_in, gate_b, pool_w, pool_b, pool_scale, conv_w, conv_b, dt_bias, a_log, d_skip, ssd_norm, rel_bias, p_pool, p_ssd, p_attn, w_out, ln2_g, ln2_b, ffn2_w13, ffn2_w2, ln3_g, ln3_b):
    sec = [w_in[:, :, IN_OFFS[i]:IN_OFFS[i + 1]] for i in range(len(IN_SIZES))]
    w_u, w_z, w_xbc, w_dt, w_q, w_k, w_v, w_g = sec
    w_pool = _pad_groups(w_u, 2).astype(BF16)
    w_z = w_z.astype(BF16)
    w_xbc = w_xbc.astype(BF16)
    w_dt = jnp.pad(w_dt, ((0, 0), (0, 0), (0, DT_PAD - SSD_HEADS))).astype(BF16)
    w_qkv = jnp.concatenate(
        [s[:, :, gi * ATTN_OUT:(gi + 1) * ATTN_OUT] for gi in range(len(ATTN_CONFIGS)) for s in (w_q, w_k, w_v)],
        axis=2).astype(BF16)
    w_g = w_g.astype(BF16)

    pw = jnp.pad(pool_w, ((0, 0), (0, 0), (0, POOL_GPAD - POOL_GDIM), (0, POOL_GPAD - POOL_GDIM))).astype(BF16)
    pb = jnp.pad(pool_b, ((0, 0), (0, 0), (0, POOL_GPAD - POOL_GDIM)))
    ps = jnp.pad(pool_scale.reshape(DEPTH, POOL_GROUPS, POOL_GDIM), ((0, 0), (0, 0), (0, POOL_GPAD - POOL_GDIM)))
    p_pool_b = _pad_groups(p_pool, 1).astype(BF16)
    p_ssd_b = p_ssd.astype(BF16)
    p_attn_b = p_attn.astype(BF16)
    w_out_b = w_out.astype(BF16)

    dtb = jnp.pad(dt_bias, ((0, 0), (0, DT_PAD - SSD_HEADS)))[:, None, :]
    alog = jnp.pad(a_log, ((0, 0), (0, DT_PAD - SSD_HEADS)))[:, None, :]
    dsk = jnp.repeat(d_skip, SSD_HEADDIM, axis=1)[:, None, :]
    cb = conv_b[:, None, :]
    nw = ssd_norm[:, None, :]
    bias = _attn_bias(rel_bias)

    f1_w13, f1_w2 = ffn1_w13.astype(BF16), ffn1_w2.astype(BF16)
    f2_w13, f2_w2 = ffn2_w13.astype(BF16), ffn2_w2.astype(BF16)
    vec = lambda a: a[:, None, :]

    h = x.reshape(N_TOK, D_MODEL)
    for i in range(DEPTH):
        h, hb = _ffn(h, f1_w13, f1_w2, vec(ln1_g), vec(ln1_b), i, True)
        ya = _pool(hb, w_pool, pw, pb, ps, i)
        yb = _ssd(hb, w_z, w_xbc, w_dt, conv_w, cb, dtb, alog, dsk, nw, i)
        yc = _attn(hb, w_qkv, bias, i)
        h = _merge(h, ya, yb, yc, w_g, gate_b, p_pool_b, p_ssd_b, p_attn_b, w_out_b, vec(ln2_g), vec(ln2_b), i)
        (h,) = _ffn(h, f2_w13, f2_w2, vec(ln3_g), vec(ln3_b), i, False)
    return h.reshape(BATCH, SEQ, D_MODEL)
```

```python
import functools

import numpy as np
import jax
import jax.numpy as jnp
from jax import lax
from jax.experimental import pallas as pl
from jax.experimental.pallas import tpu as pltpu

F32 = jnp.float32
BF16 = jnp.bfloat16

D_MODEL = 1024
BATCH = 16
SEQ = 2048
DEPTH = 4
N_TOK = BATCH * SEQ
LN_EPS = 1e-5
DN_ALPHA = (2.0 * DEPTH) ** 0.25
FFN_RES = 0.5
D_FF = ((8 * D_MODEL // 3 + 127) // 128) * 128
POOL_WIDTH = 3 * D_MODEL // 4
POOL_WINDOWS = (2, 4, 8, 16)
POOL_GROUPS = len(POOL_WINDOWS)
POOL_GDIM = POOL_WIDTH // POOL_GROUPS
SSD_INNER = D_MODEL
SSD_HEADDIM = 64
SSD_HEADS = SSD_INNER // SSD_HEADDIM
SSD_GROUPS = 4
SSD_HPG = SSD_HEADS // SSD_GROUPS
SSD_STATE = 128
SSD_CONV = 4
SSD_CHUNK = 128
SSD_CONV_CH = SSD_INNER + 2 * SSD_GROUPS * SSD_STATE
SSD_EPS = 1e-5
ATTN_CONFIGS = ((128, 1), (512, 4), (2048, 16))
ATTN_HEAD_DIM = 64
ATTN_GROUP_HEADS = 4
ATTN_HEADS = ATTN_GROUP_HEADS * len(ATTN_CONFIGS)
ATTN_WIDTH = ATTN_HEADS * ATTN_HEAD_DIM
ATTN_OUT = ATTN_GROUP_HEADS * ATTN_HEAD_DIM
ATTN_SPAN = 128
REL_BUCKETS = 32
REL_MAX_DIST = 2048
N_BRANCH = 3
IN_SIZES = (POOL_WIDTH, SSD_INNER, SSD_CONV_CH, SSD_HEADS, ATTN_WIDTH, ATTN_WIDTH, ATTN_WIDTH, N_BRANCH * D_MODEL)
IN_OFFS = tuple(sum(IN_SIZES[:i]) for i in range(len(IN_SIZES) + 1))

LANES = 128
MXU_DIM = 256
VMEM_LIMIT = 56 * 1024 * 1024

TM = 512
FFN_CHUNK = MXU_DIM
POOL_GPAD = MXU_DIM
POOL_PAD = POOL_GROUPS * POOL_GPAD
POOL_HALO = 16
POOL_ROWS = 256
DT_PAD = LANES
CONV_HALO = 8
SSD_TS = 512
NEG_BIG = -1e30

assert all(w // d == ATTN_SPAN for w, d in ATTN_CONFIGS)
assert POOL_HALO >= max(POOL_WINDOWS) and CONV_HALO >= SSD_CONV - 1

_NT = (((1,), (1,)), ((), ()))
_TN = (((0,), (0,)), ((), ()))


def _dot(a, b):
    return jnp.dot(a, b, preferred_element_type=F32)


def _silu(a):
    return a / (1.0 + jnp.exp(-a))


def _sigmoid(a):
    return 1.0 / (1.0 + jnp.exp(-a))


def _layer_norm(r, g, b):
    mu = jnp.mean(r, axis=-1, keepdims=True)
    c = r - mu
    var = jnp.mean(c * c, axis=-1, keepdims=True)
    return c * lax.rsqrt(var + LN_EPS) * g + b


def _layer_spec(block, layer):
    nd = len(block)
    return pl.BlockSpec((None,) + tuple(block), lambda *_: (layer,) + (0,) * nd,
                        pipeline_mode=pl.Buffered(1))


def _params(*sem):
    return pltpu.CompilerParams(dimension_semantics=sem, vmem_limit_bytes=VMEM_LIMIT)


def _ffn_body(x_ref, w13_ref, w2_ref, g_ref, b_ref, *rest, emit_bf16):
    if emit_bf16:
        o_ref, ob_ref, h_ref = rest
    else:
        o_ref, h_ref = rest
    x = x_ref[...]
    xb = x.astype(BF16)
    for c in range(D_FF // FFN_CHUNK):
        lo = c * FFN_CHUNK
        a = _dot(xb, w13_ref[:, lo:lo + FFN_CHUNK])
        g = _dot(xb, w13_ref[:, D_FF + lo:D_FF + lo + FFN_CHUNK])
        h_ref[:, lo:lo + FFN_CHUNK] = (_silu(a) * g).astype(BF16)
    y = _dot(h_ref[...], w2_ref[...])
    out = _layer_norm(DN_ALPHA * x + FFN_RES * y, g_ref[...], b_ref[...])
    o_ref[...] = out
    if emit_bf16:
        ob_ref[...] = out.astype(BF16)


def _ffn(x, w13, w2, g, b, layer, emit_bf16):
    row = pl.BlockSpec((TM, D_MODEL), lambda t: (t, 0))
    out_shape = [jax.ShapeDtypeStruct((N_TOK, D_MODEL), F32)]
    out_specs = [row]
    if emit_bf16:
        out_shape.append(jax.ShapeDtypeStruct((N_TOK, D_MODEL), BF16))
        out_specs.append(row)
    return pl.pallas_call(
        functools.partial(_ffn_body, emit_bf16=emit_bf16),
        grid=(N_TOK // TM,),
        in_specs=[row,
                  _layer_spec((D_MODEL, 2 * D_FF), layer),
                  _layer_spec((D_FF, D_MODEL), layer),
                  _layer_spec((1, D_MODEL), layer),
                  _layer_spec((1, D_MODEL), layer)],
        out_specs=out_specs,
        out_shape=out_shape,
        scratch_shapes=[pltpu.VMEM((TM, D_FF), BF16)],
        compiler_params=_params("parallel"),
        name="ffn",
    )(x, w13, w2, g, b)


def _pool_body(xb_ref, w_ref, pw_ref, pb_ref, ps_ref, o_ref, u_ref):
    u_ref[0:POOL_HALO, :] = jnp.zeros((POOL_HALO, POOL_PAD), F32)

    def proj(j, carry):
        r0 = pl.multiple_of(j * TM, TM)
        u_ref[pl.ds(POOL_HALO + r0, TM), :] = _dot(xb_ref[pl.ds(r0, TM), :], w_ref[...])
        return carry

    lax.fori_loop(0, SEQ // TM, proj, 0)

    def chunk(c, carry):
        r0 = pl.multiple_of(c * POOL_ROWS, POOL_ROWS)
        t = r0 + lax.broadcasted_iota(jnp.int32, (POOL_ROWS, 1), 0)
        for g, win in enumerate(POOL_WINDOWS):
            cols = slice(g * POOL_GPAD, (g + 1) * POOL_GPAD)
            ug = u_ref[pl.ds(r0, POOL_ROWS + POOL_HALO), cols]
            s = ug + pltpu.roll(ug, 1, 0)
            k = 2
            while k < win:
                s = s + pltpu.roll(s, k, 0)
                k *= 2
            inv = 1.0 / jnp.minimum(t + 1, win).astype(F32)
            pooled = s[POOL_HALO:] * inv - ug[POOL_HALO:]
            y = _dot(pooled.astype(BF16), pw_ref[g])
            y = (y + pb_ref[g:g + 1, :]) * ps_ref[g:g + 1, :]
            o_ref[pl.ds(r0, POOL_ROWS), cols] = y.astype(BF16)
        return carry

    lax.fori_loop(0, SEQ // POOL_ROWS, chunk, 0)


def _pool(xb, w, pw, pb, ps, layer):
    seq = pl.BlockSpec((SEQ, D_MODEL), lambda b: (b, 0))
    return pl.pallas_call(
        _pool_body,
        grid=(BATCH,),
        in_specs=[seq,
                  _layer_spec((D_MODEL, POOL_PAD), layer),
                  _layer_spec((POOL_GROUPS, POOL_GPAD, POOL_GPAD), layer),
                  _layer_spec((POOL_GROUPS, POOL_GPAD), layer),
                  _layer_spec((POOL_GROUPS, POOL_GPAD), layer)],
        out_specs=pl.BlockSpec((SEQ, POOL_PAD), lambda b: (b, 0)),
        out_shape=jax.ShapeDtypeStruct((N_TOK, POOL_PAD), BF16),
        scratch_shapes=[pltpu.VMEM((POOL_HALO + SEQ, POOL_PAD), F32)],
        compiler_params=_params("parallel"),
        name="pool_mixer",
    )(xb, w, pw, pb, ps)


def _split3_bf16(v):
    hi = v.astype(BF16)
    r1 = v - hi.astype(F32)
    mid = r1.astype(BF16)
    lo = (r1 - mid.astype(F32)).astype(BF16)
    return hi, mid, lo


def _softplus(v):
    return jnp.maximum(v, 0.0) + jnp.log1p(jnp.exp(-jnp.abs(v)))


def _ssd_body(xb_ref, wz_ref, wx_ref, wdt_ref, wdtx_ref, cw_ref, cb_ref, dtb_ref, dtbx_ref, alog_ref, dsk_ref, nw_ref,
              o_ref, pad_ref, xc_ref, dtx_ref, adt_ref, y_ref, st_ref):
    @pl.when(pl.program_id(1) == 0)
    def _():
        st_ref[...] = jnp.zeros(st_ref.shape, F32)
        pad_ref[0:CONV_HALO, :] = jnp.zeros((CONV_HALO, SSD_CONV_CH), F32)

    xb = xb_ref[...]
    pad_ref[CONV_HALO:CONV_HALO + SSD_TS, :] = _dot(xb, wx_ref[...])

    def conv(k, carry):
        r0 = pl.multiple_of(k * SSD_CHUNK, SSD_CHUNK)
        for c in range(SSD_CONV_CH // LANES):
            cols = slice(c * LANES, (c + 1) * LANES)
            blk = pad_ref[pl.ds(r0, CONV_HALO + SSD_CHUNK), cols]
            acc = cw_ref[0:1, cols] * blk
            for j in range(1, SSD_CONV):
                acc = cw_ref[j:j + 1, cols] * blk + pltpu.roll(acc, 1, 0)
            xc_ref[pl.ds(r0, SSD_CHUNK), cols] = _silu(acc[CONV_HALO:] + cb_ref[:, cols])
        return carry

    lax.fori_loop(0, SSD_TS // SSD_CHUNK, conv, 0)
    pad_ref[0:CONV_HALO, :] = pad_ref[SSD_TS:SSD_TS + CONV_HALO, :]

    adt_ref[...] = _softplus(_dot(xb, wdt_ref[...]) + dtb_ref[...]) * (-jnp.exp(alog_ref[...]))
    dtx_ref[...] = _softplus(_dot(xb, wdtx_ref[...]) + dtbx_ref[...])

    li = lax.broadcasted_iota(jnp.int32, (SSD_CHUNK, SSD_CHUNK), 0)
    si = lax.broadcasted_iota(jnp.int32, (SSD_CHUNK, SSD_CHUNK), 1)
    tril = li >= si
    ones_tril = jnp.where(tril, 1.0, 0.0).astype(BF16)
    first_head = si < SSD_HEADDIM
    eh = lax.broadcasted_iota(jnp.int32, (DT_PAD, SSD_INNER), 0)
    ec = lax.broadcasted_iota(jnp.int32, (DT_PAD, SSD_INNER), 1)
    expand = jnp.where(ec // SSD_HEADDIM == eh, 1.0, 0.0).astype(BF16)
    gw = SSD_HPG * SSD_HEADDIM

    def chunk(k, carry):
        r0 = pl.multiple_of(k * SSD_CHUNK, SSD_CHUNK)
        rows = pl.ds(r0, SSD_CHUNK)
        hi, mid, lo = _split3_bf16(adt_ref[rows, :])
        acum = _dot(ones_tril, hi) + _dot(ones_tril, mid) + _dot(ones_tril, lo)
        acum_t = acum.T
        hi, mid, lo = _split3_bf16(acum)
        acx = _dot(hi, expand) + _dot(mid, expand) + _dot(lo, expand)
        a_last = acx[SSD_CHUNK - 1:SSD_CHUNK, :]
        xdt = xc_ref[rows, 0:SSD_INNER] * dtx_ref[rows, :]
        xdt_b = xdt.astype(BF16)
        xw_b = (xdt * jnp.exp(a_last - acx)).astype(BF16)
        e_cum = jnp.exp(acx)
        e_last = jnp.exp(a_last)
        for g in range(SSD_GROUPS):
            gc = slice(g * gw, (g + 1) * gw)
            b_lo = SSD_INNER + g * SSD_STATE
            c_lo = SSD_INNER + SSD_GROUPS * SSD_STATE + g * SSD_STATE
            bmb = xc_ref[rows, b_lo:b_lo + SSD_STATE].astype(BF16)
            cmb = xc_ref[rows, c_lo:c_lo + SSD_STATE].astype(BF16)
            cbm = lax.dot_general(cmb, bmb, _NT, preferred_element_type=F32)
            prev = st_ref[g]
            y_off = _dot(cmb, prev.astype(BF16)) * e_cum[:, gc]
            new = lax.dot_general(bmb, xw_b[:, gc], _TN, preferred_element_type=F32)
            st_ref[g] = prev * e_last[:, gc] + new
            y_diag = []
            for t in range(gw // LANES):
                x_pair = xdt_b[:, g * gw + t * LANES:g * gw + (t + 1) * LANES]
                outs = []
                for e in range(LANES // SSD_HEADDIM):
                    h = g * SSD_HPG + t * (LANES // SSD_HEADDIM) + e
                    diff = acum[:, h:h + 1] - acum_t[h:h + 1, :]
                    decay = jnp.exp(jnp.where(tril, diff, -jnp.inf))
                    outs.append(_dot((cbm * decay).astype(BF16), x_pair))
                y_diag.append(jnp.where(first_head, outs[0], outs[1]))
            y_ref[rows, gc] = jnp.concatenate(y_diag, axis=1) + y_off
        return carry

    lax.fori_loop(0, SSD_TS // SSD_CHUNK, chunk, 0)

    y = y_ref[...] + dsk_ref[...] * xc_ref[:, 0:SSD_INNER]
    y = y * _silu(_dot(xb, wz_ref[...]))
    for g in range(SSD_GROUPS):
        cols = slice(g * gw, (g + 1) * gw)
        yg = y[:, cols]
        ms = jnp.mean(yg * yg, axis=-1, keepdims=True)
        o_ref[:, cols] = (yg * lax.rsqrt(ms + SSD_EPS) * nw_ref[:, cols]).astype(BF16)


def _ssd(xb, wz, wx, wdt, wdtx, cw, cb, dtb, dtbx, alog, dsk, nw, layer):
    steps = SEQ // SSD_TS
    row = pl.BlockSpec((SSD_TS, D_MODEL), lambda b, s: (b * steps + s, 0))
    return pl.pallas_call(
        _ssd_body,
        grid=(BATCH, steps),
        in_specs=[row,
                  _layer_spec((D_MODEL, SSD_INNER), layer),
                  _layer_spec((D_MODEL, SSD_CONV_CH), layer),
                  _layer_spec((D_MODEL, DT_PAD), layer),
                  _layer_spec((D_MODEL, SSD_INNER), layer),
                  _layer_spec((SSD_CONV, SSD_CONV_CH), layer),
                  _layer_spec((1, SSD_CONV_CH), layer),
                  _layer_spec((1, DT_PAD), layer),
                  _layer_spec((1, SSD_INNER), layer),
                  _layer_spec((1, DT_PAD), layer),
                  _layer_spec((1, SSD_INNER), layer),
                  _layer_spec((1, SSD_INNER), layer)],
        out_specs=pl.BlockSpec((SSD_TS, SSD_INNER), lambda b, s: (b * steps + s, 0)),
        out_shape=jax.ShapeDtypeStruct((N_TOK, SSD_INNER), BF16),
        scratch_shapes=[pltpu.VMEM((CONV_HALO + SSD_TS, SSD_CONV_CH), F32),
                        pltpu.VMEM((SSD_TS, SSD_CONV_CH), F32),
                        pltpu.VMEM((SSD_TS, SSD_INNER), F32),
                        pltpu.VMEM((SSD_TS, DT_PAD), F32),
                        pltpu.VMEM((SSD_TS, SSD_INNER), F32),
                        pltpu.VMEM((SSD_GROUPS, SSD_STATE, SSD_HPG * SSD_HEADDIM), F32)],
        compiler_params=_params("arbitrary", "arbitrary"),
        name="ssd_mixer",
    )(xb, wz, wx, wdt, wdtx, cw, cb, dtb, dtbx, alog, dsk, nw)


def _attn_body(xb_ref, w_ref, bias_ref, o_ref, tile_ref, qh_ref, kb_ref, vb_ref, op_ref, lp_ref, oa_ref, la_ref):
    n_lt = ATTN_OUT // LANES
    hpt = LANES // ATTN_HEAD_DIM
    lane = lax.broadcasted_iota(jnp.int32, (ATTN_SPAN, LANES), 1)
    first_head = lane < ATTN_HEAD_DIM
    prev_cols = lax.broadcasted_iota(jnp.int32, (1, 2 * ATTN_SPAN), 1) < ATTN_SPAN
    for t in range(n_lt):
        kb_ref[t, 0:ATTN_SPAN, :] = jnp.zeros((ATTN_SPAN, LANES), BF16)
        vb_ref[t, 0:ATTN_SPAN, 0:LANES] = jnp.zeros((ATTN_SPAN, LANES), BF16)
        vb_ref[t, :, LANES:2 * LANES] = jnp.ones((ATTN_SPAN + SEQ, LANES), BF16)

    for g, (_, dil) in enumerate(ATTN_CONFIGS):
        sub = SEQ // dil
        n_blk = sub // ATTN_SPAN
        per = TM // dil
        wcol = g * 3 * ATTN_OUT
        first_rows = lax.broadcasted_iota(jnp.int32, (per, LANES), 1) < ATTN_HEAD_DIM

        for j in range(SEQ // TM):
            qkv = _dot(xb_ref[j * TM:(j + 1) * TM, :], w_ref[:, wcol:wcol + 3 * ATTN_OUT])
            for which in range(3):
                for t in range(n_lt):
                    c0 = which * ATTN_OUT + t * LANES
                    piece = qkv[:, c0:c0 + LANES]
                    if which == 0:
                        piece = piece * (ATTN_HEAD_DIM ** -0.5)
                    if dil > 1:
                        tile_ref[which * n_lt + t] = piece
                    for r in range(dil):
                        if dil > 1:
                            piece = tile_ref[which * n_lt + t, pl.ds(r, per, stride=dil), :]
                        if which == 0:
                            dst = pl.ds(r * sub + j * per, per)
                            qh_ref[t * hpt, dst, :] = jnp.where(first_rows, piece, 0.0).astype(BF16)
                            qh_ref[t * hpt + 1, dst, :] = jnp.where(first_rows, 0.0, piece).astype(BF16)
                        elif which == 1:
                            kb_ref[t, pl.ds(ATTN_SPAN + r * sub + j * per, per), :] = piece.astype(BF16)
                        else:
                            vb_ref[t, pl.ds(ATTN_SPAN + r * sub + j * per, per), 0:LANES] = piece.astype(BF16)

        def block(u, carry, g=g, n_blk=n_blk, dil=dil):
            r0 = pl.multiple_of(u * ATTN_SPAN, ATTN_SPAN)
            rows = pl.ds(r0, ATTN_SPAN)
            krows = pl.ds(r0, 2 * ATTN_SPAN)
            prev_mask = jnp.where(prev_cols, jnp.where(u % n_blk == 0, NEG_BIG, 0.0).astype(F32), 0.0)
            for t in range(n_lt):
                k2 = kb_ref[t, krows, :]
                v2 = vb_ref[t, krows, :]
                outs, lses = [], []
                for e in range(hpt):
                    h = g * ATTN_GROUP_HEADS + t * hpt + e
                    s = (lax.dot_general(qh_ref[t * hpt + e, rows, :], k2, _NT, preferred_element_type=F32)
                         + bias_ref[h] + prev_mask)
                    m = jnp.max(s, axis=-1, keepdims=True)
                    pv = _dot(jnp.exp(s - m).astype(BF16), v2)
                    den = pv[:, LANES:2 * LANES]
                    outs.append(pv[:, 0:LANES] / den)
                    lses.append(m + jnp.log(den))
                o_pair = jnp.where(first_head, outs[0], outs[1])
                l_pair = jnp.where(first_head, lses[0], lses[1])
                if dil == 1:
                    oa_ref[t, rows, :] = o_pair
                    la_ref[t, rows, :] = l_pair
                else:
                    op_ref[t, rows, :] = o_pair
                    lp_ref[t, rows, :] = l_pair
            return carry

        lax.fori_loop(0, SEQ // ATTN_SPAN, block, 0)

        if dil > 1:
            for t in range(n_lt):
                for r in range(dil):
                    nat = pl.ds(r, sub, stride=dil)
                    o_g = op_ref[t, r * sub:(r + 1) * sub, :]
                    l_g = lp_ref[t, r * sub:(r + 1) * sub, :]
                    o_a = oa_ref[t, nat, :]
                    l_a = la_ref[t, nat, :]
                    mx = jnp.maximum(l_a, l_g)
                    l_n = mx + jnp.log(jnp.exp(l_a - mx) + jnp.exp(l_g - mx))
                    oa_ref[t, nat, :] = o_a * jnp.exp(l_a - l_n) + o_g * jnp.exp(l_g - l_n)
                    la_ref[t, nat, :] = l_n

    for t in range(n_lt):
        o_ref[:, t * LANES:(t + 1) * LANES] = oa_ref[t].astype(BF16)


def _attn(xb, w, bias, layer):
    n_lt = ATTN_OUT // LANES
    return pl.pallas_call(
        _attn_body,
        grid=(BATCH,),
        in_specs=[pl.BlockSpec((SEQ, D_MODEL), lambda b: (b, 0)),
                  _layer_spec((D_MODEL, 3 * ATTN_WIDTH), layer),
                  pl.BlockSpec((ATTN_HEADS, ATTN_SPAN, 2 * ATTN_SPAN), lambda b: (0, 0, 0),
                               pipeline_mode=pl.Buffered(1))],
        out_specs=pl.BlockSpec((SEQ, ATTN_OUT), lambda b: (b, 0)),
        out_shape=jax.ShapeDtypeStruct((N_TOK, ATTN_OUT), BF16),
        scratch_shapes=[pltpu.VMEM((3 * n_lt, TM, LANES), F32),
                        pltpu.VMEM((ATTN_GROUP_HEADS, SEQ, LANES), BF16),
                        pltpu.VMEM((n_lt, ATTN_SPAN + SEQ, LANES), BF16),
                        pltpu.VMEM((n_lt, ATTN_SPAN + SEQ, 2 * LANES), BF16),
                        pltpu.VMEM((n_lt, SEQ, LANES), F32),
                        pltpu.VMEM((n_lt, SEQ, LANES), F32),
                        pltpu.VMEM((n_lt, SEQ, LANES), F32),
                        pltpu.VMEM((n_lt, SEQ, LANES), F32)],
        compiler_params=_params("parallel"),
        name="dilated_attn",
    )(xb, w, bias)


def _merge_body(x_ref, ya_ref, yb_ref, yc_ref, wg_ref, gb_ref, pa_ref, pb_ref, pc_ref, wo_ref, g_ref, b_ref, o_ref):
    x = x_ref[...]
    xb = x.astype(BF16)
    merged = None
    for i, (y_ref, p_ref) in enumerate(((ya_ref, pa_ref), (yb_ref, pb_ref), (yc_ref, pc_ref))):
        gate = _sigmoid(_dot(xb, wg_ref[:, i * D_MODEL:(i + 1) * D_MODEL]) + gb_ref[i:i + 1, :])
        term = gate * _dot(y_ref[...], p_ref[...])
        merged = term if merged is None else merged + term
    mix = _dot(merged.astype(BF16), wo_ref[...])
    o_ref[...] = _layer_norm(DN_ALPHA * x + mix, g_ref[...], b_ref[...])


def _merge(x, ya, yb, yc, wg, gb, pa, pb, pc, wo, g, b, layer):
    def row(width):
        return pl.BlockSpec((TM, width), lambda t: (t, 0))

    return pl.pallas_call(
        _merge_body,
        grid=(N_TOK // TM,),
        in_specs=[row(D_MODEL), row(POOL_PAD), row(SSD_INNER), row(ATTN_OUT),
                  _layer_spec((D_MODEL, N_BRANCH * D_MODEL), layer),
                  _layer_spec((N_BRANCH, D_MODEL), layer),
                  _layer_spec((POOL_PAD, D_MODEL), layer),
                  _layer_spec((SSD_INNER, D_MODEL), layer),
                  _layer_spec((ATTN_OUT, D_MODEL), layer),
                  _layer_spec((D_MODEL, D_MODEL), layer),
                  _layer_spec((1, D_MODEL), layer),
                  _layer_spec((1, D_MODEL), layer)],
        out_specs=row(D_MODEL),
        out_shape=jax.ShapeDtypeStruct((N_TOK, D_MODEL), F32),
        compiler_params=_params("parallel"),
        name="gated_merge",
    )(x, ya, yb, yc, wg, gb, pa, pb, pc, wo, g, b)


def _t5_bucket(dist):
    dist = np.maximum(dist, 0)
    max_exact = REL_BUCKETS // 2
    large = max_exact + (np.log(np.maximum(dist, 1) / max_exact) / np.log(REL_MAX_DIST / max_exact)
                         * (REL_BUCKETS - max_exact)).astype(np.int32)
    large = np.minimum(large, REL_BUCKETS - 1)
    return np.where(dist < max_exact, dist, large).astype(np.int32)


def _attn_bias(rel_bias):
    qi = np.arange(ATTN_SPAN)[:, None]
    kj = np.arange(2 * ATTN_SPAN)[None, :]
    delta = qi - kj + ATTN_SPAN
    in_band = jnp.asarray((delta >= 0) & (delta <= ATTN_SPAN))
    mats = []
    for gi, (_, dil) in enumerate(ATTN_CONFIGS):
        tab = rel_bias[:, gi * ATTN_GROUP_HEADS:(gi + 1) * ATTN_GROUP_HEADS].astype(F32)
        b = jnp.take(tab, jnp.asarray(_t5_bucket(delta * dil)), axis=0)
        b = jnp.where(in_band[..., None], b, NEG_BIG)
        mats.append(b.transpose(2, 0, 1))
    return jnp.concatenate(mats, axis=0)


def _pad_groups(a, axis):
    shp = a.shape
    a = a.reshape(shp[:axis] + (POOL_GROUPS, POOL_GDIM) + shp[axis + 1:])
    pad = [(0, 0)] * a.ndim
    pad[axis + 1] = (0, POOL_GPAD - POOL_GDIM)
    a = jnp.pad(a, pad)
    return a.reshape(shp[:axis] + (POOL_PAD,) + shp[axis + 1:])


def kernel(x, ffn1_w13, ffn1_w2, ln1_g, ln1_b, w_in, gate_b, pool_w, pool_b, pool_scale, conv_w, conv_b, dt_bias, a_log, d_skip, ssd_norm, rel_bias, p_pool, p_ssd, p_attn, w_out, ln2_g, ln2_b, ffn2_w13, ffn2_w2, ln3_g, ln3_b):
    sec = [w_in[:, :, IN_OFFS[i]:IN_OFFS[i + 1]] for i in range(len(IN_SIZES))]
    w_u, w_z, w_xbc, w_dt, w_q, w_k, w_v, w_g = sec
    w_pool = _pad_groups(w_u, 2).astype(BF16)
    w_z = w_z.astype(BF16)
    w_xbc = w_xbc.astype(BF16)
    w_dtx = jnp.repeat(w_dt, SSD_HEADDIM, axis=2).astype(BF16)
    w_dt = jnp.pad(w_dt, ((0, 0), (0, 0), (0, DT_PAD - SSD_HEADS))).astype(BF16)
    w_qkv = jnp.concatenate(
        [s[:, :, gi * ATTN_OUT:(gi + 1) * ATTN_OUT] for gi in range(len(ATTN_CONFIGS)) for s in (w_q, w_k, w_v)],
        axis=2).astype(BF16)
    w_g = w_g.astype(BF16)

    pw = jnp.pad(pool_w, ((0, 0), (0, 0), (0, POOL_GPAD - POOL_GDIM), (0, POOL_GPAD - POOL_GDIM))).astype(BF16)
    pb = jnp.pad(pool_b, ((0, 0), (0, 0), (0, POOL_GPAD - POOL_GDIM)))
    ps = jnp.pad(pool_scale.reshape(DEPTH, POOL_GROUPS, POOL_GDIM), ((0, 0), (0, 0), (0, POOL_GPAD - POOL_GDIM)))
    p_pool_b = _pad_groups(p_pool, 1).astype(BF16)
    p_ssd_b = p_ssd.astype(BF16)
    p_attn_b = p_attn.astype(BF16)
    w_out_b = w_out.astype(BF16)

    dtbx = jnp.repeat(dt_bias, SSD_HEADDIM, axis=1)[:, None, :]
    dtb = jnp.pad(dt_bias, ((0, 0), (0, DT_PAD - SSD_HEADS)))[:, None, :]
    alog = jnp.pad(a_log, ((0, 0), (0, DT_PAD - SSD_HEADS)))[:, None, :]
    dsk = jnp.repeat(d_skip, SSD_HEADDIM, axis=1)[:, None, :]
    cb = conv_b[:, None, :]
    nw = ssd_norm[:, None, :]
    bias = _attn_bias(rel_bias)

    f1_w13, f1_w2 = ffn1_w13.astype(BF16), ffn1_w2.astype(BF16)
    f2_w13, f2_w2 = ffn2_w13.astype(BF16), ffn2_w2.astype(BF16)
    vec = lambda a: a[:, None, :]

    h = x.reshape(N_TOK, D_MODEL)
    for i in range(DEPTH):
        h, hb = _ffn(h, f1_w13, f1_w2, vec(ln1_g), vec(ln1_b), i, True)
        ya = _pool(hb, w_pool, pw, pb, ps, i)
        yb = _ssd(hb, w_z, w_xbc, w_dt, w_dtx, conv_w, cb, dtb, dtbx, alog, dsk, nw, i)
        yc = _attn(hb, w_qkv, bias, i)
        h = _merge(h, ya, yb, yc, w_g, gate_b, p_pool_b, p_ssd_b, p_attn_b, w_out_b, vec(ln2_g), vec(ln2_b), i)
        (h,) = _ffn(h, f2_w13, f2_w2, vec(ln3_g), vec(ln3_b), i, False)
    return h.reshape(BATCH, SEQ, D_MODEL)
```

```python
import functools

import numpy as np
import jax
import jax.numpy as jnp
from jax import lax
from jax.experimental import pallas as pl
from jax.experimental.pallas import tpu as pltpu

F32 = jnp.float32
BF16 = jnp.bfloat16

D_MODEL = 1024
BATCH = 16
SEQ = 2048
DEPTH = 4
N_TOK = BATCH * SEQ
LN_EPS = 1e-5
DN_ALPHA = (2.0 * DEPTH) ** 0.25
FFN_RES = 0.5
D_FF = ((8 * D_MODEL // 3 + 127) // 128) * 128
POOL_WIDTH = 3 * D_MODEL // 4
POOL_WINDOWS = (2, 4, 8, 16)
POOL_GROUPS = len(POOL_WINDOWS)
POOL_GDIM = POOL_WIDTH // POOL_GROUPS
SSD_INNER = D_MODEL
SSD_HEADDIM = 64
SSD_HEADS = SSD_INNER // SSD_HEADDIM
SSD_GROUPS = 4
SSD_HPG = SSD_HEADS // SSD_GROUPS
SSD_STATE = 128
SSD_CONV = 4
SSD_CHUNK = 128
SSD_CONV_CH = SSD_INNER + 2 * SSD_GROUPS * SSD_STATE
SSD_EPS = 1e-5
ATTN_CONFIGS = ((128, 1), (512, 4), (2048, 16))
ATTN_HEAD_DIM = 64
ATTN_GROUP_HEADS = 4
ATTN_HEADS = ATTN_GROUP_HEADS * len(ATTN_CONFIGS)
ATTN_WIDTH = ATTN_HEADS * ATTN_HEAD_DIM
ATTN_OUT = ATTN_GROUP_HEADS * ATTN_HEAD_DIM
ATTN_SPAN = 128
REL_BUCKETS = 32
REL_MAX_DIST = 2048
N_BRANCH = 3
IN_SIZES = (POOL_WIDTH, SSD_INNER, SSD_CONV_CH, SSD_HEADS, ATTN_WIDTH, ATTN_WIDTH, ATTN_WIDTH, N_BRANCH * D_MODEL)
IN_OFFS = tuple(sum(IN_SIZES[:i]) for i in range(len(IN_SIZES) + 1))

LANES = 128
MXU_DIM = 256
VMEM_LIMIT = 56 * 1024 * 1024

TM = 512
FFN_CHUNK = MXU_DIM
POOL_GPAD = MXU_DIM
POOL_PAD = POOL_GROUPS * POOL_GPAD
POOL_HALO = 16
POOL_ROWS = 256
DT_PAD = LANES
CONV_HALO = 8
SSD_TS = 512
NEG_BIG = -1e30
ATTN_UNROLL = 4

assert all(w // d == ATTN_SPAN for w, d in ATTN_CONFIGS)
assert POOL_HALO >= max(POOL_WINDOWS) and CONV_HALO >= SSD_CONV - 1

_NT = (((1,), (1,)), ((), ()))
_TN = (((0,), (0,)), ((), ()))


def _dot(a, b):
    return jnp.dot(a, b, preferred_element_type=F32)


def _silu(a):
    return a / (1.0 + jnp.exp(-a))


def _sigmoid(a):
    return 1.0 / (1.0 + jnp.exp(-a))


def _layer_norm(r, g, b):
    mu = jnp.mean(r, axis=-1, keepdims=True)
    c = r - mu
    var = jnp.mean(c * c, axis=-1, keepdims=True)
    return c * lax.rsqrt(var + LN_EPS) * g + b


def _layer_spec(block, layer):
    nd = len(block)
    return pl.BlockSpec((None,) + tuple(block), lambda *_: (layer,) + (0,) * nd,
                        pipeline_mode=pl.Buffered(1))


def _params(*sem):
    return pltpu.CompilerParams(dimension_semantics=sem, vmem_limit_bytes=VMEM_LIMIT)


def _ffn_body(x_ref, w13_ref, w2_ref, g_ref, b_ref, *rest, emit_bf16):
    if emit_bf16:
        o_ref, ob_ref, h_ref = rest
    else:
        o_ref, h_ref = rest
    x = x_ref[...]
    xb = x.astype(BF16)
    for c in range(D_FF // FFN_CHUNK):
        lo = c * FFN_CHUNK
        a = _dot(xb, w13_ref[:, lo:lo + FFN_CHUNK])
        g = _dot(xb, w13_ref[:, D_FF + lo:D_FF + lo + FFN_CHUNK])
        h_ref[:, lo:lo + FFN_CHUNK] = (_silu(a) * g).astype(BF16)
    y = _dot(h_ref[...], w2_ref[...])
    out = _layer_norm(DN_ALPHA * x + FFN_RES * y, g_ref[...], b_ref[...])
    o_ref[...] = out
    if emit_bf16:
        ob_ref[...] = out.astype(BF16)


def _ffn(x, w13, w2, g, b, layer, emit_bf16):
    row = pl.BlockSpec((TM, D_MODEL), lambda t: (t, 0))
    out_shape = [jax.ShapeDtypeStruct((N_TOK, D_MODEL), F32)]
    out_specs = [row]
    if emit_bf16:
        out_shape.append(jax.ShapeDtypeStruct((N_TOK, D_MODEL), BF16))
        out_specs.append(row)
    return pl.pallas_call(
        functools.partial(_ffn_body, emit_bf16=emit_bf16),
        grid=(N_TOK // TM,),
        in_specs=[row,
                  _layer_spec((D_MODEL, 2 * D_FF), layer),
                  _layer_spec((D_FF, D_MODEL), layer),
                  _layer_spec((1, D_MODEL), layer),
                  _layer_spec((1, D_MODEL), layer)],
        out_specs=out_specs,
        out_shape=out_shape,
        scratch_shapes=[pltpu.VMEM((TM, D_FF), BF16)],
        compiler_params=_params("parallel"),
        name="ffn",
    )(x, w13, w2, g, b)


def _pool_body(xb_ref, w_ref, pw_ref, pb_ref, ps_ref, o_ref, u_ref):
    u_ref[0:POOL_HALO, :] = jnp.zeros((POOL_HALO, POOL_PAD), F32)

    def proj(j, carry):
        r0 = pl.multiple_of(j * TM, TM)
        u_ref[pl.ds(POOL_HALO + r0, TM), :] = _dot(xb_ref[pl.ds(r0, TM), :], w_ref[...])
        return carry

    lax.fori_loop(0, SEQ // TM, proj, 0)

    def chunk(c, carry):
        r0 = pl.multiple_of(c * POOL_ROWS, POOL_ROWS)
        t = r0 + lax.broadcasted_iota(jnp.int32, (POOL_ROWS, 1), 0)
        for g, win in enumerate(POOL_WINDOWS):
            cols = slice(g * POOL_GPAD, (g + 1) * POOL_GPAD)
            ug = u_ref[pl.ds(r0, POOL_ROWS + POOL_HALO), cols]
            s = ug + pltpu.roll(ug, 1, 0)
            k = 2
            while k < win:
                s = s + pltpu.roll(s, k, 0)
                k *= 2
            inv = 1.0 / jnp.minimum(t + 1, win).astype(F32)
            pooled = s[POOL_HALO:] * inv - ug[POOL_HALO:]
            y = _dot(pooled.astype(BF16), pw_ref[g])
            y = (y + pb_ref[g:g + 1, :]) * ps_ref[g:g + 1, :]
            o_ref[pl.ds(r0, POOL_ROWS), cols] = y.astype(BF16)
        return carry

    lax.fori_loop(0, SEQ // POOL_ROWS, chunk, 0)


def _pool(xb, w, pw, pb, ps, layer):
    seq = pl.BlockSpec((SEQ, D_MODEL), lambda b: (b, 0))
    return pl.pallas_call(
        _pool_body,
        grid=(BATCH,),
        in_specs=[seq,
                  _layer_spec((D_MODEL, POOL_PAD), layer),
                  _layer_spec((POOL_GROUPS, POOL_GPAD, POOL_GPAD), layer),
                  _layer_spec((POOL_GROUPS, POOL_GPAD), layer),
                  _layer_spec((POOL_GROUPS, POOL_GPAD), layer)],
        out_specs=pl.BlockSpec((SEQ, POOL_PAD), lambda b: (b, 0)),
        out_shape=jax.ShapeDtypeStruct((N_TOK, POOL_PAD), BF16),
        scratch_shapes=[pltpu.VMEM((POOL_HALO + SEQ, POOL_PAD), F32)],
        compiler_params=_params("parallel"),
        name="pool_mixer",
    )(xb, w, pw, pb, ps)


def _split3_bf16(v):
    hi = v.astype(BF16)
    r1 = v - hi.astype(F32)
    mid = r1.astype(BF16)
    lo = (r1 - mid.astype(F32)).astype(BF16)
    return hi, mid, lo


def _softplus(v):
    return jnp.maximum(v, 0.0) + jnp.log1p(jnp.exp(-jnp.abs(v)))


def _ssd_body(xb_ref, wz_ref, wx_ref, wdt_ref, ex_ref, cw_ref, cb_ref, dtb_ref, alog_ref, dsk_ref, nw_ref,
              o_ref, pad_ref, xc_ref, dtx_ref, adt_ref, y_ref, st_ref):
    @pl.when(pl.program_id(1) == 0)
    def _():
        st_ref[...] = jnp.zeros(st_ref.shape, F32)
        pad_ref[0:CONV_HALO, :] = jnp.zeros((CONV_HALO, SSD_CONV_CH), F32)

    def per_head_lanes(v):
        return _dot(jnp.concatenate(_split3_bf16(v), axis=1), ex_ref[...])

    xb = xb_ref[...]
    pad_ref[CONV_HALO:CONV_HALO + SSD_TS, :] = _dot(xb, wx_ref[...])

    def conv(k):
        r0 = k * SSD_CHUNK
        for c in range(SSD_CONV_CH // LANES):
            cols = slice(c * LANES, (c + 1) * LANES)
            blk = pad_ref[pl.ds(r0, CONV_HALO + SSD_CHUNK), cols]
            acc = cw_ref[0:1, cols] * blk
            for j in range(1, SSD_CONV):
                acc = cw_ref[j:j + 1, cols] * blk + pltpu.roll(acc, 1, 0)
            xc_ref[pl.ds(r0, SSD_CHUNK), cols] = _silu(acc[CONV_HALO:] + cb_ref[:, cols])

    for k in range(SSD_TS // SSD_CHUNK):
        conv(k)
    pad_ref[0:CONV_HALO, :] = pad_ref[SSD_TS:SSD_TS + CONV_HALO, :]

    dt = _softplus(_dot(xb, wdt_ref[...]) + dtb_ref[...])
    adt_ref[...] = dt * (-jnp.exp(alog_ref[...]))
    dtx_ref[...] = per_head_lanes(dt)

    li = lax.broadcasted_iota(jnp.int32, (SSD_CHUNK, SSD_CHUNK), 0)
    si = lax.broadcasted_iota(jnp.int32, (SSD_CHUNK, SSD_CHUNK), 1)
    tril = li >= si
    ones_tril = jnp.where(tril, 1.0, 0.0).astype(BF16)
    first_head = si < SSD_HEADDIM
    gw = SSD_HPG * SSD_HEADDIM

    def chunk(k):
        rows = pl.ds(k * SSD_CHUNK, SSD_CHUNK)
        parts = _dot(ones_tril, jnp.concatenate(_split3_bf16(adt_ref[rows, :]), axis=1))
        acum = (parts[:, 0:DT_PAD] + parts[:, DT_PAD:2 * DT_PAD]) + parts[:, 2 * DT_PAD:3 * DT_PAD]
        acum_t = acum.T
        acx = per_head_lanes(acum)
        a_last = acx[SSD_CHUNK - 1:SSD_CHUNK, :]
        xdt = xc_ref[rows, 0:SSD_INNER] * dtx_ref[rows, :]
        xw_b = (xdt * jnp.exp(a_last - acx)).astype(BF16)
        e_cum = jnp.exp(acx)
        e_last = jnp.exp(a_last)
        for g in range(SSD_GROUPS):
            gc = slice(g * gw, (g + 1) * gw)
            b_lo = SSD_INNER + g * SSD_STATE
            c_lo = SSD_INNER + SSD_GROUPS * SSD_STATE + g * SSD_STATE
            bmb = xc_ref[rows, b_lo:b_lo + SSD_STATE].astype(BF16)
            cmb = xc_ref[rows, c_lo:c_lo + SSD_STATE].astype(BF16)
            cbm = lax.dot_general(cmb, bmb, _NT, preferred_element_type=F32)
            prev = st_ref[g]
            y_off = _dot(cmb, prev.astype(BF16)) * e_cum[:, gc]
            new = lax.dot_general(bmb, xw_b[:, gc], _TN, preferred_element_type=F32)
            st_ref[g] = prev * e_last[:, gc] + new
            y_diag = []
            for t in range(gw // LANES):
                x_pair = xdt[:, g * gw + t * LANES:g * gw + (t + 1) * LANES]
                mix = []
                for e in range(LANES // SSD_HEADDIM):
                    h = g * SSD_HPG + t * (LANES // SSD_HEADDIM) + e
                    diff = acum[:, h:h + 1] - acum_t[h:h + 1, :]
                    decay = jnp.exp(jnp.where(tril, diff, -jnp.inf))
                    mix.append((cbm * decay).astype(BF16))
                x_split = jnp.concatenate([jnp.where(first_head, x_pair, 0.0).astype(BF16),
                                           jnp.where(first_head, 0.0, x_pair).astype(BF16)], axis=0)
                y_diag.append(_dot(jnp.concatenate(mix, axis=1), x_split))
            y_ref[rows, gc] = jnp.concatenate(y_diag, axis=1) + y_off

    for k in range(SSD_TS // SSD_CHUNK):
        chunk(k)

    y = y_ref[...] + dsk_ref[...] * xc_ref[:, 0:SSD_INNER]
    y = y * _silu(_dot(xb, wz_ref[...]))
    for g in range(SSD_GROUPS):
        cols = slice(g * gw, (g + 1) * gw)
        yg = y[:, cols]
        ms = jnp.mean(yg * yg, axis=-1, keepdims=True)
        o_ref[:, cols] = (yg * lax.rsqrt(ms + SSD_EPS) * nw_ref[:, cols]).astype(BF16)


def _head_expander():
    e = (np.arange(SSD_INNER)[None, :] // SSD_HEADDIM == np.arange(DT_PAD)[:, None]).astype(np.float32)
    return jnp.asarray(np.concatenate([e, e, e], axis=0), BF16)


def _ssd(xb, wz, wx, wdt, cw, cb, dtb, alog, dsk, nw, layer):
    steps = SEQ // SSD_TS
    row = pl.BlockSpec((SSD_TS, D_MODEL), lambda b, s: (b * steps + s, 0))
    return pl.pallas_call(
        _ssd_body,
        grid=(BATCH, steps),
        in_specs=[row,
                  _layer_spec((D_MODEL, SSD_INNER), layer),
                  _layer_spec((D_MODEL, SSD_CONV_CH), layer),
                  _layer_spec((D_MODEL, DT_PAD), layer),
                  pl.BlockSpec((3 * DT_PAD, SSD_INNER), lambda b, s: (0, 0), pipeline_mode=pl.Buffered(1)),
                  _layer_spec((SSD_CONV, SSD_CONV_CH), layer),
                  _layer_spec((1, SSD_CONV_CH), layer),
                  _layer_spec((1, DT_PAD), layer),
                  _layer_spec((1, DT_PAD), layer),
                  _layer_spec((1, SSD_INNER), layer),
                  _layer_spec((1, SSD_INNER), layer)],
        out_specs=pl.BlockSpec((SSD_TS, SSD_INNER), lambda b, s: (b * steps + s, 0)),
        out_shape=jax.ShapeDtypeStruct((N_TOK, SSD_INNER), BF16),
        scratch_shapes=[pltpu.VMEM((CONV_HALO + SSD_TS, SSD_CONV_CH), F32),
                        pltpu.VMEM((SSD_TS, SSD_CONV_CH), F32),
                        pltpu.VMEM((SSD_TS, SSD_INNER), F32),
                        pltpu.VMEM((SSD_TS, DT_PAD), F32),
                        pltpu.VMEM((SSD_TS, SSD_INNER), F32),
                        pltpu.VMEM((SSD_GROUPS, SSD_STATE, SSD_HPG * SSD_HEADDIM), F32)],
        compiler_params=_params("arbitrary", "arbitrary"),
        name="ssd_mixer",
    )(xb, wz, wx, wdt, _head_expander(), cw, cb, dtb, alog, dsk, nw)


def _attn_body(xb_ref, w_ref, bias_ref, o_ref, tile_ref, qh_ref, kb_ref, vb_ref,
               ng_ref, mg_ref, dg_ref, na_ref, ma_ref, da_ref):
    n_lt = ATTN_OUT // LANES
    hpt = LANES // ATTN_HEAD_DIM
    lane = lax.broadcasted_iota(jnp.int32, (ATTN_SPAN, LANES), 1)
    first_head = lane < ATTN_HEAD_DIM
    for t in range(n_lt):
        kb_ref[t, 0:ATTN_SPAN, :] = jnp.zeros((ATTN_SPAN, LANES), BF16)
        vb_ref[t, 0:ATTN_SPAN, 0:LANES] = jnp.zeros((ATTN_SPAN, LANES), BF16)
        vb_ref[t, :, LANES:2 * LANES] = jnp.ones((ATTN_SPAN + SEQ, LANES), BF16)

    for g, (_, dil) in enumerate(ATTN_CONFIGS):
        sub = SEQ // dil
        n_blk = sub // ATTN_SPAN
        per = TM // dil
        wcol = g * 3 * ATTN_OUT
        first_rows = lax.broadcasted_iota(jnp.int32, (per, LANES), 1) < ATTN_HEAD_DIM

        for j in range(SEQ // TM):
            qkv = _dot(xb_ref[j * TM:(j + 1) * TM, :], w_ref[:, wcol:wcol + 3 * ATTN_OUT])
            for which in range(3):
                for t in range(n_lt):
                    c0 = which * ATTN_OUT + t * LANES
                    piece = qkv[:, c0:c0 + LANES]
                    if which == 0:
                        piece = piece * (ATTN_HEAD_DIM ** -0.5)
                    if dil > 1:
                        tile_ref[which * n_lt + t] = piece
                    for r in range(dil):
                        if dil > 1:
                            piece = tile_ref[which * n_lt + t, pl.ds(r, per, stride=dil), :]
                        if which == 0:
                            dst = pl.ds(r * sub + j * per, per)
                            qh_ref[t * hpt, dst, :] = jnp.where(first_rows, piece, 0.0).astype(BF16)
                            qh_ref[t * hpt + 1, dst, :] = jnp.where(first_rows, 0.0, piece).astype(BF16)
                        elif which == 1:
                            kb_ref[t, pl.ds(ATTN_SPAN + r * sub + j * per, per), :] = piece.astype(BF16)
                        else:
                            vb_ref[t, pl.ds(ATTN_SPAN + r * sub + j * per, per), 0:LANES] = piece.astype(BF16)

        def block(u, carry, g=g, n_blk=n_blk, dil=dil):
            r0 = pl.multiple_of(u * ATTN_SPAN, ATTN_SPAN)
            rows = pl.ds(r0, ATTN_SPAN)
            krows = pl.ds(r0, 2 * ATTN_SPAN)
            no_prev = ATTN_HEADS if n_blk == 1 else jnp.where(u % n_blk == 0, ATTN_HEADS, 0)
            dst = (na_ref, ma_ref, da_ref) if dil == 1 else (ng_ref, mg_ref, dg_ref)
            for t in range(n_lt):
                k2 = kb_ref[t, krows, :]
                v2 = vb_ref[t, krows, :]
                parts = []
                for e in range(hpt):
                    h = g * ATTN_GROUP_HEADS + t * hpt + e
                    s = (lax.dot_general(qh_ref[t * hpt + e, rows, :], k2, _NT, preferred_element_type=F32)
                         + bias_ref[h + no_prev])
                    m = jnp.max(s, axis=-1, keepdims=True)
                    pv = _dot(jnp.exp(s - m).astype(BF16), v2)
                    parts.append((pv[:, 0:LANES], jnp.broadcast_to(m, (ATTN_SPAN, LANES)), pv[:, LANES:2 * LANES]))
                for ref, a, b in zip(dst, parts[0], parts[1]):
                    ref[t, rows, :] = jnp.where(first_head, a, b)
            return carry

        lax.fori_loop(0, SEQ // ATTN_SPAN, block, 0, unroll=ATTN_UNROLL)

        if dil > 1:
            for t in range(n_lt):
                for r in range(dil):
                    nat = pl.ds(r, sub, stride=dil)
                    grp = slice(r * sub, (r + 1) * sub)
                    m_a = ma_ref[t, nat, :]
                    m_g = mg_ref[t, grp, :]
                    m_n = jnp.maximum(m_a, m_g)
                    e_a = jnp.exp(m_a - m_n)
                    e_g = jnp.exp(m_g - m_n)
                    na_ref[t, nat, :] = na_ref[t, nat, :] * e_a + ng_ref[t, grp, :] * e_g
                    da_ref[t, nat, :] = da_ref[t, nat, :] * e_a + dg_ref[t, grp, :] * e_g
                    ma_ref[t, nat, :] = m_n

    for t in range(n_lt):
        o_ref[:, t * LANES:(t + 1) * LANES] = (na_ref[t] / da_ref[t]).astype(BF16)


def _attn(xb, w, bias, layer):
    n_lt = ATTN_OUT // LANES
    return pl.pallas_call(
        _attn_body,
        grid=(BATCH,),
        in_specs=[pl.BlockSpec((SEQ, D_MODEL), lambda b: (b, 0)),
                  _layer_spec((D_MODEL, 3 * ATTN_WIDTH), layer),
                  pl.BlockSpec((2 * ATTN_HEADS, ATTN_SPAN, 2 * ATTN_SPAN), lambda b: (0, 0, 0),
                               pipeline_mode=pl.Buffered(1))],
        out_specs=pl.BlockSpec((SEQ, ATTN_OUT), lambda b: (b, 0)),
        out_shape=jax.ShapeDtypeStruct((N_TOK, ATTN_OUT), BF16),
        scratch_shapes=[pltpu.VMEM((3 * n_lt, TM, LANES), F32),
                        pltpu.VMEM((ATTN_GROUP_HEADS, SEQ, LANES), BF16),
                        pltpu.VMEM((n_lt, ATTN_SPAN + SEQ, LANES), BF16),
                        pltpu.VMEM((n_lt, ATTN_SPAN + SEQ, 2 * LANES), BF16)]
                       + [pltpu.VMEM((n_lt, SEQ, LANES), F32)] * 6,
        compiler_params=_params("parallel"),
        name="dilated_attn",
    )(xb, w, bias)


def _merge_body(x_ref, ya_ref, yb_ref, yc_ref, wg_ref, gb_ref, pa_ref, pb_ref, pc_ref, wo_ref, g_ref, b_ref, o_ref):
    x = x_ref[...]
    xb = x.astype(BF16)
    merged = None
    for i, (y_ref, p_ref) in enumerate(((ya_ref, pa_ref), (yb_ref, pb_ref), (yc_ref, pc_ref))):
        gate = _sigmoid(_dot(xb, wg_ref[:, i * D_MODEL:(i + 1) * D_MODEL]) + gb_ref[i:i + 1, :])
        term = gate * _dot(y_ref[...], p_ref[...])
        merged = term if merged is None else merged + term
    mix = _dot(merged.astype(BF16), wo_ref[...])
    o_ref[...] = _layer_norm(DN_ALPHA * x + mix, g_ref[...], b_ref[...])


def _merge(x, ya, yb, yc, wg, gb, pa, pb, pc, wo, g, b, layer):
    def row(width):
        return pl.BlockSpec((TM, width), lambda t: (t, 0))

    return pl.pallas_call(
        _merge_body,
        grid=(N_TOK // TM,),
        in_specs=[row(D_MODEL), row(POOL_PAD), row(SSD_INNER), row(ATTN_OUT),
                  _layer_spec((D_MODEL, N_BRANCH * D_MODEL), layer),
                  _layer_spec((N_BRANCH, D_MODEL), layer),
                  _layer_spec((POOL_PAD, D_MODEL), layer),
                  _layer_spec((SSD_INNER, D_MODEL), layer),
                  _layer_spec((ATTN_OUT, D_MODEL), layer),
                  _layer_spec((D_MODEL, D_MODEL), layer),
                  _layer_spec((1, D_MODEL), layer),
                  _layer_spec((1, D_MODEL), layer)],
        out_specs=row(D_MODEL),
        out_shape=jax.ShapeDtypeStruct((N_TOK, D_MODEL), F32),
        compiler_params=_params("parallel"),
        name="gated_merge",
    )(x, ya, yb, yc, wg, gb, pa, pb, pc, wo, g, b)


def _t5_bucket(dist):
    dist = np.maximum(dist, 0)
    max_exact = REL_BUCKETS // 2
    large = max_exact + (np.log(np.maximum(dist, 1) / max_exact) / np.log(REL_MAX_DIST / max_exact)
                         * (REL_BUCKETS - max_exact)).astype(np.int32)
    large = np.minimum(large, REL_BUCKETS - 1)
    return np.where(dist < max_exact, dist, large).astype(np.int32)


def _attn_bias(rel_bias):
    qi = np.arange(ATTN_SPAN)[:, None]
    kj = np.arange(2 * ATTN_SPAN)[None, :]
    delta = qi - kj + ATTN_SPAN
    in_band = (delta >= 0) & (delta <= ATTN_SPAN)
    mats = []
    for gi, (_, dil) in enumerate(ATTN_CONFIGS):
        tab = rel_bias[:, gi * ATTN_GROUP_HEADS:(gi + 1) * ATTN_GROUP_HEADS].astype(F32)
        bucket = jnp.asarray(np.where(in_band, _t5_bucket(delta * dil), -1))
        onehot = (bucket[..., None] == jnp.arange(REL_BUCKETS)).astype(F32)
        b = jnp.einsum('qkb,bh->hqk', onehot, tab, precision=lax.Precision.HIGHEST)
        mats.append(jnp.where(jnp.asarray(in_band), b, NEG_BIG))
    bias = jnp.concatenate(mats, axis=0)
    prev_cols = jnp.asarray(kj < ATTN_SPAN)
    return jnp.concatenate([bias, jnp.where(prev_cols, NEG_BIG, bias)], axis=0)


def _pad_groups(a, axis):
    shp = a.shape
    a = a.reshape(shp[:axis] + (POOL_GROUPS, POOL_GDIM) + shp[axis + 1:])
    pad = [(0, 0)] * a.ndim
    pad[axis + 1] = (0, POOL_GPAD - POOL_GDIM)
    a = jnp.pad(a, pad)
    return a.reshape(shp[:axis] + (POOL_PAD,) + shp[axis + 1:])


def kernel(x, ffn1_w13, ffn1_w2, ln1_g, ln1_b, w_in, gate_b, pool_w, pool_b, pool_scale, conv_w, conv_b, dt_bias, a_log, d_skip, ssd_norm, rel_bias, p_pool, p_ssd, p_attn, w_out, ln2_g, ln2_b, ffn2_w13, ffn2_w2, ln3_g, ln3_b):
    sec = [w_in[:, :, IN_OFFS[i]:IN_OFFS[i + 1]] for i in range(len(IN_SIZES))]
    w_u, w_z, w_xbc, w_dt, w_q, w_k, w_v, w_g = sec
    w_pool = _pad_groups(w_u, 2).astype(BF16)
    w_z = w_z.astype(BF16)
    w_xbc = w_xbc.astype(BF16)
    w_dt = jnp.pad(w_dt, ((0, 0), (0, 0), (0, DT_PAD - SSD_HEADS))).astype(BF16)
    w_qkv = jnp.concatenate(
        [s[:, :, gi * ATTN_OUT:(gi + 1) * ATTN_OUT] for gi in range(len(ATTN_CONFIGS)) for s in (w_q, w_k, w_v)],
        axis=2).astype(BF16)
    w_g = w_g.astype(BF16)

    pw = jnp.pad(pool_w, ((0, 0), (0, 0), (0, POOL_GPAD - POOL_GDIM), (0, POOL_GPAD - POOL_GDIM))).astype(BF16)
    pb = jnp.pad(pool_b, ((0, 0), (0, 0), (0, POOL_GPAD - POOL_GDIM)))
    ps = jnp.pad(pool_scale.reshape(DEPTH, POOL_GROUPS, POOL_GDIM), ((0, 0), (0, 0), (0, POOL_GPAD - POOL_GDIM)))
    p_pool_b = _pad_groups(p_pool, 1).astype(BF16)
    p_ssd_b = p_ssd.astype(BF16)
    p_attn_b = p_attn.astype(BF16)
    w_out_b = w_out.astype(BF16)

    dtb = jnp.pad(dt_bias, ((0, 0), (0, DT_PAD - SSD_HEADS)))[:, None, :]
    alog = jnp.pad(a_log, ((0, 0), (0, DT_PAD - SSD_HEADS)))[:, None, :]
    dsk = jnp.repeat(d_skip, SSD_HEADDIM, axis=1)[:, None, :]
    cb = conv_b[:, None, :]
    nw = ssd_norm[:, None, :]
    bias = _attn_bias(rel_bias)

    f1_w13, f1_w2 = ffn1_w13.astype(BF16), ffn1_w2.astype(BF16)
    f2_w13, f2_w2 = ffn2_w13.astype(BF16), ffn2_w2.astype(BF16)
    vec = lambda a: a[:, None, :]

    h = x.reshape(N_TOK, D_MODEL)
    for i in range(DEPTH):
        h, hb = _ffn(h, f1_w13, f1_w2, vec(ln1_g), vec(ln1_b), i, True)
        ya = _pool(hb, w_pool, pw, pb, ps, i)
        yb = _ssd(hb, w_z, w_xbc, w_dt, conv_w, cb, dtb, alog, dsk, nw, i)
        yc = _attn(hb, w_qkv, bias, i)
        h = _merge(h, ya, yb, yc, w_g, gate_b, p_pool_b, p_ssd_b, p_attn_b, w_out_b, vec(ln2_g), vec(ln2_b), i)
        (h,) = _ffn(h, f2_w13, f2_w2, vec(ln3_g), vec(ln3_b), i, False)
    return h.reshape(BATCH, SEQ, D_MODEL)
```

```python
import functools

import numpy as np
import jax
import jax.numpy as jnp
from jax import lax
from jax.experimental import pallas as pl
from jax.experimental.pallas import tpu as pltpu

F32 = jnp.float32
BF16 = jnp.bfloat16

D_MODEL = 1024
BATCH = 16
SEQ = 2048
DEPTH = 4
N_TOK = BATCH * SEQ
LN_EPS = 1e-5
DN_ALPHA = (2.0 * DEPTH) ** 0.25
FFN_RES = 0.5
D_FF = ((8 * D_MODEL // 3 + 127) // 128) * 128
POOL_WIDTH = 3 * D_MODEL // 4
POOL_WINDOWS = (2, 4, 8, 16)
POOL_GROUPS = len(POOL_WINDOWS)
POOL_GDIM = POOL_WIDTH // POOL_GROUPS
SSD_INNER = D_MODEL
SSD_HEADDIM = 64
SSD_HEADS = SSD_INNER // SSD_HEADDIM
SSD_GROUPS = 4
SSD_HPG = SSD_HEADS // SSD_GROUPS
SSD_STATE = 128
SSD_CONV = 4
SSD_CHUNK = 128
SSD_CONV_CH = SSD_INNER + 2 * SSD_GROUPS * SSD_STATE
SSD_EPS = 1e-5
ATTN_CONFIGS = ((128, 1), (512, 4), (2048, 16))
ATTN_HEAD_DIM = 64
ATTN_GROUP_HEADS = 4
ATTN_HEADS = ATTN_GROUP_HEADS * len(ATTN_CONFIGS)
ATTN_WIDTH = ATTN_HEADS * ATTN_HEAD_DIM
ATTN_OUT = ATTN_GROUP_HEADS * ATTN_HEAD_DIM
ATTN_SPAN = 128
REL_BUCKETS = 32
REL_MAX_DIST = 2048
N_BRANCH = 3
IN_SIZES = (POOL_WIDTH, SSD_INNER, SSD_CONV_CH, SSD_HEADS, ATTN_WIDTH, ATTN_WIDTH, ATTN_WIDTH, N_BRANCH * D_MODEL)
IN_OFFS = tuple(sum(IN_SIZES[:i]) for i in range(len(IN_SIZES) + 1))

LANES = 128
MXU_DIM = 256
VMEM_LIMIT = 56 * 1024 * 1024

TM = 512
FFN_CHUNK = MXU_DIM
POOL_GPAD = MXU_DIM
POOL_PAD = POOL_GROUPS * POOL_GPAD
POOL_HALO = 16
POOL_ROWS = 256
DT_PAD = LANES
CONV_HALO = 8
SSD_TS = 512
CONV_STRIDE = 4
CONV_UNIT = 8 * CONV_STRIDE
NEG_BIG = -1e30
ATTN_UNROLL = 4
ATTN_ORDER = (2, 1, 0)
assert ATTN_CONFIGS[ATTN_ORDER[-1]][1] == 1

assert all(w // d == ATTN_SPAN for w, d in ATTN_CONFIGS)
assert POOL_HALO >= max(POOL_WINDOWS) and CONV_HALO >= SSD_CONV - 1

_NT = (((1,), (1,)), ((), ()))
_TN = (((0,), (0,)), ((), ()))


def _dot(a, b):
    return jnp.dot(a, b, preferred_element_type=F32)


def _silu(a):
    h = 0.5 * a
    return h + h * jnp.tanh(h)


def _sigmoid(a):
    return 1.0 / (1.0 + jnp.exp(-a))


def _layer_norm(r, g, b):
    mu = jnp.mean(r, axis=-1, keepdims=True)
    c = r - mu
    var = jnp.mean(c * c, axis=-1, keepdims=True)
    return c * lax.rsqrt(var + LN_EPS) * g + b


def _layer_spec(block, layer):
    nd = len(block)
    return pl.BlockSpec((None,) + tuple(block), lambda *_: (layer,) + (0,) * nd,
                        pipeline_mode=pl.Buffered(1))


def _params(*sem):
    return pltpu.CompilerParams(dimension_semantics=sem, vmem_limit_bytes=VMEM_LIMIT)


def _ffn_body(x_ref, w13_ref, w2_ref, g_ref, b_ref, *rest, emit_bf16):
    if emit_bf16:
        o_ref, ob_ref, h_ref = rest
    else:
        o_ref, h_ref = rest
    x = x_ref[...]
    xb = x.astype(BF16)
    for c in range(D_FF // FFN_CHUNK):
        lo = c * FFN_CHUNK
        a = _dot(xb, w13_ref[:, lo:lo + FFN_CHUNK])
        g = _dot(xb, w13_ref[:, D_FF + lo:D_FF + lo + FFN_CHUNK])
        h_ref[:, lo:lo + FFN_CHUNK] = (_silu(a) * g).astype(BF16)
    y = _dot(h_ref[...], w2_ref[...])
    out = _layer_norm(DN_ALPHA * x + FFN_RES * y, g_ref[...], b_ref[...])
    o_ref[...] = out
    if emit_bf16:
        ob_ref[...] = out.astype(BF16)


def _ffn(x, w13, w2, g, b, layer, emit_bf16):
    row = pl.BlockSpec((TM, D_MODEL), lambda t: (t, 0))
    out_shape = [jax.ShapeDtypeStruct((N_TOK, D_MODEL), F32)]
    out_specs = [row]
    if emit_bf16:
        out_shape.append(jax.ShapeDtypeStruct((N_TOK, D_MODEL), BF16))
        out_specs.append(row)
    return pl.pallas_call(
        functools.partial(_ffn_body, emit_bf16=emit_bf16),
        grid=(N_TOK // TM,),
        in_specs=[row,
                  _layer_spec((D_MODEL, 2 * D_FF), layer),
                  _layer_spec((D_FF, D_MODEL), layer),
                  _layer_spec((1, D_MODEL), layer),
                  _layer_spec((1, D_MODEL), layer)],
        out_specs=out_specs,
        out_shape=out_shape,
        scratch_shapes=[pltpu.VMEM((TM, D_FF), BF16)],
        compiler_params=_params("parallel"),
        name="ffn",
    )(x, w13, w2, g, b)


def _pool_body(xb_ref, w_ref, pw_ref, pb_ref, ps_ref, o_ref, u_ref):
    u_ref[0:POOL_HALO, :] = jnp.zeros((POOL_HALO, POOL_PAD), F32)

    def proj(j, carry):
        r0 = pl.multiple_of(j * TM, TM)
        u_ref[pl.ds(POOL_HALO + r0, TM), :] = _dot(xb_ref[pl.ds(r0, TM), :], w_ref[...])
        return carry

    lax.fori_loop(0, SEQ // TM, proj, 0)

    def chunk(c, carry):
        r0 = pl.multiple_of(c * POOL_ROWS, POOL_ROWS)
        t = r0 + lax.broadcasted_iota(jnp.int32, (POOL_ROWS, 1), 0)
        for g, win in enumerate(POOL_WINDOWS):
            cols = slice(g * POOL_GPAD, (g + 1) * POOL_GPAD)
            ug = u_ref[pl.ds(r0, POOL_ROWS + POOL_HALO), cols]
            s = ug + pltpu.roll(ug, 1, 0)
            k = 2
            while k < win:
                s = s + pltpu.roll(s, k, 0)
                k *= 2
            inv = 1.0 / jnp.minimum(t + 1, win).astype(F32)
            pooled = s[POOL_HALO:] * inv - ug[POOL_HALO:]
            y = _dot(pooled.astype(BF16), pw_ref[g])
            y = (y + pb_ref[g:g + 1, :]) * ps_ref[g:g + 1, :]
            o_ref[pl.ds(r0, POOL_ROWS), cols] = y.astype(BF16)
        return carry

    lax.fori_loop(0, SEQ // POOL_ROWS, chunk, 0)


def _pool(xb, w, pw, pb, ps, layer):
    seq = pl.BlockSpec((SEQ, D_MODEL), lambda b: (b, 0))
    return pl.pallas_call(
        _pool_body,
        grid=(BATCH,),
        in_specs=[seq,
                  _layer_spec((D_MODEL, POOL_PAD), layer),
                  _layer_spec((POOL_GROUPS, POOL_GPAD, POOL_GPAD), layer),
                  _layer_spec((POOL_GROUPS, POOL_GPAD), layer),
                  _layer_spec((POOL_GROUPS, POOL_GPAD), layer)],
        out_specs=pl.BlockSpec((SEQ, POOL_PAD), lambda b: (b, 0)),
        out_shape=jax.ShapeDtypeStruct((N_TOK, POOL_PAD), BF16),
        scratch_shapes=[pltpu.VMEM((POOL_HALO + SEQ, POOL_PAD), F32)],
        compiler_params=_params("parallel"),
        name="pool_mixer",
    )(xb, w, pw, pb, ps)


def _split3_bf16(v):
    hi = v.astype(BF16)
    r1 = v - hi.astype(F32)
    mid = r1.astype(BF16)
    lo = (r1 - mid.astype(F32)).astype(BF16)
    return hi, mid, lo


def _softplus(v):
    return jnp.maximum(v, 0.0) + jnp.log1p(jnp.exp(-jnp.abs(v)))


def _interleaved_time(i):
    return (i & ~(CONV_UNIT - 1)) | ((i & 7) << 2) | ((i >> 3) & (CONV_STRIDE - 1))


def _ssd_body(xb_ref, wz_ref, wx_ref, wdt_ref, ex_ref, cw_ref, cb_ref, dtb_ref, alog_ref, dsk_ref, nw_ref,
              o_ref, pad_ref, xc_ref, dt_ref, acum_ref, ex2_ref, y_ref, st_ref):
    n_ct = SSD_CONV_CH // LANES
    n_chunks = SSD_TS // SSD_CHUNK
    units = [(u, v) for u in range(SSD_CHUNK // CONV_UNIT) for v in range(CONV_STRIDE)]

    @pl.when(pl.program_id(1) == 0)
    def _():
        st_ref[...] = jnp.zeros(st_ref.shape, F32)
        for c in range(n_ct):
            pad_ref[c, 0:CONV_HALO, :] = jnp.zeros((CONV_HALO, LANES), F32)

    def interleaved(ref_rows, base):
        return jnp.concatenate([ref_rows(pl.ds(base + u * CONV_UNIT + v, 8, stride=CONV_STRIDE))
                                for u, v in units], axis=0)

    xb = xb_ref[...]
    raw = _dot(xb, wx_ref[...])
    for c in range(n_ct):
        pad_ref[c, CONV_HALO:CONV_HALO + SSD_TS, :] = raw[:, c * LANES:(c + 1) * LANES]

    def conv(k):
        for c in range(n_ct):
            cols = slice(c * LANES, (c + 1) * LANES)
            acc = None
            for j in range(SSD_CONV):
                tap = jnp.broadcast_to(cw_ref[j:j + 1, cols], (SSD_CHUNK, LANES))
                base = CONV_HALO + k * SSD_CHUNK - (SSD_CONV - 1) + j
                term = tap * interleaved(lambda rows, c=c: pad_ref[c, rows, :], base)
                acc = term if acc is None else acc + term
            xc_ref[pl.ds(k * SSD_CHUNK, SSD_CHUNK), cols] = _silu(acc + cb_ref[:, cols])

    for k in range(n_chunks):
        conv(k)
    for c in range(n_ct):
        pad_ref[c, 0:CONV_HALO, :] = pad_ref[c, SSD_TS:SSD_TS + CONV_HALO, :]

    li = _interleaved_time(lax.broadcasted_iota(jnp.int32, (SSD_CHUNK, SSD_CHUNK), 0))
    si = _interleaved_time(lax.broadcasted_iota(jnp.int32, (SSD_CHUNK, SSD_CHUNK), 1))
    tril = li >= si
    ones_tril = jnp.where(tril, 1.0, 0.0).astype(BF16)
    first_head = lax.broadcasted_iota(jnp.int32, (SSD_CHUNK, LANES), 1) < SSD_HEADDIM
    gw = SSD_HPG * SSD_HEADDIM

    dt_ref[...] = _softplus(_dot(xb, wdt_ref[...]) + dtb_ref[...])
    a_neg = -jnp.exp(alog_ref[...])
    dts = []
    for k in range(n_chunks):
        dt_k = interleaved(lambda rows: dt_ref[rows, :], k * SSD_CHUNK)
        parts = _dot(ones_tril, jnp.concatenate(_split3_bf16(dt_k * a_neg), axis=1))
        acum_ref[pl.ds(k * SSD_CHUNK, SSD_CHUNK), :] = (
            (parts[:, 0:DT_PAD] + parts[:, DT_PAD:2 * DT_PAD]) + parts[:, 2 * DT_PAD:3 * DT_PAD])
        dts.append(dt_k)
    both = jnp.concatenate(dts + [acum_ref[...]], axis=0)
    ex2_ref[...] = _dot(jnp.concatenate(_split3_bf16(both), axis=1), ex_ref[...])

    def chunk(k):
        rows = pl.ds(k * SSD_CHUNK, SSD_CHUNK)
        acum = acum_ref[rows, :]
        acum_t = acum.T
        acx = ex2_ref[pl.ds(SSD_TS + k * SSD_CHUNK, SSD_CHUNK), :]
        a_last = acx[SSD_CHUNK - 1:SSD_CHUNK, :]
        xs = xc_ref[rows, 0:SSD_INNER]
        xdt = xs * ex2_ref[rows, :]
        xw_b = (xdt * jnp.exp(a_last - acx)).astype(BF16)
        e_cum = jnp.exp(acx)
        e_last = jnp.exp(a_last)
        for g in range(SSD_GROUPS):
            gc = slice(g * gw, (g + 1) * gw)
            b_lo = SSD_INNER + g * SSD_STATE
            c_lo = SSD_INNER + SSD_GROUPS * SSD_STATE + g * SSD_STATE
            bmb = xc_ref[rows, b_lo:b_lo + SSD_STATE].astype(BF16)
            cmb = xc_ref[rows, c_lo:c_lo + SSD_STATE].astype(BF16)
            cbm = lax.dot_general(cmb, bmb, _NT, preferred_element_type=F32)
            prev = st_ref[g]
            y_off = _dot(cmb, prev.astype(BF16)) * e_cum[:, gc]
            new = lax.dot_general(bmb, xw_b[:, gc], _TN, preferred_element_type=F32)
            st_ref[g] = prev * e_last[:, gc] + new
            for t in range(gw // LANES):
                tc = slice(g * gw + t * LANES, g * gw + (t + 1) * LANES)
                x_pair = xdt[:, tc]
                mix = []
                for e in range(LANES // SSD_HEADDIM):
                    h = g * SSD_HPG + t * (LANES // SSD_HEADDIM) + e
                    diff = acum[:, h:h + 1] - acum_t[h:h + 1, :]
                    decay = jnp.exp(jnp.where(tril, diff, -jnp.inf))
                    mix.append((cbm * decay).astype(BF16))
                x_split = jnp.concatenate([jnp.where(first_head, x_pair, 0.0).astype(BF16),
                                           jnp.where(first_head, 0.0, x_pair).astype(BF16)], axis=0)
                y_t = (_dot(jnp.concatenate(mix, axis=1), x_split) + y_off[:, t * LANES:(t + 1) * LANES]
                       + dsk_ref[:, tc] * xs[:, tc])
                for n, (u, v) in enumerate(units):
                    y_ref[g * (gw // LANES) + t,
                          pl.ds(k * SSD_CHUNK + u * CONV_UNIT + v, 8, stride=CONV_STRIDE), :] = y_t[n * 8:(n + 1) * 8, :]

    for k in range(n_chunks):
        chunk(k)

    z = _dot(xb, wz_ref[...])
    for g in range(SSD_GROUPS):
        cols = slice(g * gw, (g + 1) * gw)
        yg = jnp.concatenate([y_ref[g * (gw // LANES) + t] for t in range(gw // LANES)], axis=1)
        yg = yg * _silu(z[:, cols])
        ms = jnp.mean(yg * yg, axis=-1, keepdims=True)
        o_ref[:, cols] = (yg * lax.rsqrt(ms + SSD_EPS) * nw_ref[:, cols]).astype(BF16)


def _head_expander():
    e = (np.arange(SSD_INNER)[None, :] // SSD_HEADDIM == np.arange(DT_PAD)[:, None]).astype(np.float32)
    return jnp.asarray(np.concatenate([e, e, e], axis=0), BF16)


def _ssd(xb, wz, wx, wdt, cw, cb, dtb, alog, dsk, nw, layer):
    steps = SEQ // SSD_TS
    row = pl.BlockSpec((SSD_TS, D_MODEL), lambda b, s: (b * steps + s, 0))
    return pl.pallas_call(
        _ssd_body,
        grid=(BATCH, steps),
        in_specs=[row,
                  _layer_spec((D_MODEL, SSD_INNER), layer),
                  _layer_spec((D_MODEL, SSD_CONV_CH), layer),
                  _layer_spec((D_MODEL, DT_PAD), layer),
                  pl.BlockSpec((3 * DT_PAD, SSD_INNER), lambda b, s: (0, 0), pipeline_mode=pl.Buffered(1)),
                  _layer_spec((SSD_CONV, SSD_CONV_CH), layer),
                  _layer_spec((1, SSD_CONV_CH), layer),
                  _layer_spec((1, DT_PAD), layer),
                  _layer_spec((1, DT_PAD), layer),
                  _layer_spec((1, SSD_INNER), layer),
                  _layer_spec((1, SSD_INNER), layer)],
        out_specs=pl.BlockSpec((SSD_TS, SSD_INNER), lambda b, s: (b * steps + s, 0)),
        out_shape=jax.ShapeDtypeStruct((N_TOK, SSD_INNER), BF16),
        scratch_shapes=[pltpu.VMEM((SSD_CONV_CH // LANES, CONV_HALO + SSD_TS, LANES), F32),
                        pltpu.VMEM((SSD_TS, SSD_CONV_CH), F32),
                        pltpu.VMEM((SSD_TS, DT_PAD), F32),
                        pltpu.VMEM((SSD_TS, DT_PAD), F32),
                        pltpu.VMEM((2 * SSD_TS, SSD_INNER), F32),
                        pltpu.VMEM((SSD_INNER // LANES, SSD_TS, LANES), F32),
                        pltpu.VMEM((SSD_GROUPS, SSD_STATE, SSD_HPG * SSD_HEADDIM), F32)],
        compiler_params=_params("arbitrary", "arbitrary"),
        name="ssd_mixer",
    )(xb, wz, wx, wdt, _head_expander(), cw, cb, dtb, alog, dsk, nw)


def _attn_body(xb_ref, w_ref, bias_ref, o_ref, tile_ref, qh_ref, kb_ref, vb_ref, na_ref, ma_ref, da_ref):
    n_lt = ATTN_OUT // LANES
    hpt = LANES // ATTN_HEAD_DIM
    lane = lax.broadcasted_iota(jnp.int32, (ATTN_SPAN, LANES), 1)
    first_head = lane < ATTN_HEAD_DIM
    for t in range(n_lt):
        kb_ref[t, 0:ATTN_SPAN, :] = jnp.zeros((ATTN_SPAN, LANES), BF16)
        vb_ref[t, 0:ATTN_SPAN, 0:LANES] = jnp.zeros((ATTN_SPAN, LANES), BF16)
        vb_ref[t, :, LANES:2 * LANES] = jnp.ones((ATTN_SPAN + SEQ, LANES), BF16)

    for g in ATTN_ORDER:
        dil = ATTN_CONFIGS[g][1]
        sub = SEQ // dil
        n_blk = sub // ATTN_SPAN
        per = TM // dil
        wcol = g * 3 * ATTN_OUT
        first_rows = lax.broadcasted_iota(jnp.int32, (per, LANES), 1) < ATTN_HEAD_DIM

        for j in range(SEQ // TM):
            qkv = _dot(xb_ref[j * TM:(j + 1) * TM, :], w_ref[:, wcol:wcol + 3 * ATTN_OUT])
            for which in range(3):
                for t in range(n_lt):
                    c0 = which * ATTN_OUT + t * LANES
                    piece = qkv[:, c0:c0 + LANES]
                    if which == 0:
                        piece = piece * (ATTN_HEAD_DIM ** -0.5)
                    slot = (j * 3 + which) * n_lt + t
                    if dil > 1:
                        tile_ref[slot] = piece
                    for r in range(dil):
                        if dil > 1:
                            piece = tile_ref[slot, pl.ds(r, per, stride=dil), :]
                        if which == 0:
                            dst = pl.ds(r * sub + j * per, per)
                            qh_ref[t * hpt, dst, :] = jnp.where(first_rows, piece, 0.0).astype(BF16)
                            qh_ref[t * hpt + 1, dst, :] = jnp.where(first_rows, 0.0, piece).astype(BF16)
                        elif which == 1:
                            kb_ref[t, pl.ds(ATTN_SPAN + r * sub + j * per, per), :] = piece.astype(BF16)
                        else:
                            vb_ref[t, pl.ds(ATTN_SPAN + r * sub + j * per, per), 0:LANES] = piece.astype(BF16)

        def block(u, carry, g=g, n_blk=n_blk, dil=dil):
            r0 = pl.multiple_of(u * ATTN_SPAN, ATTN_SPAN)
            rows = pl.ds(r0, ATTN_SPAN)
            krows = pl.ds(r0, 2 * ATTN_SPAN)
            no_prev = ATTN_HEADS if n_blk == 1 else jnp.where(u % n_blk == 0, ATTN_HEADS, 0)
            nat = rows if dil == 1 else pl.ds(u // n_blk + (u % n_blk) * (ATTN_SPAN * dil), ATTN_SPAN, stride=dil)
            for t in range(n_lt):
                k2 = kb_ref[t, krows, :]
                v2 = vb_ref[t, krows, :]
                parts = []
                for e in range(hpt):
                    h = g * ATTN_GROUP_HEADS + t * hpt + e
                    s = (lax.dot_general(qh_ref[t * hpt + e, rows, :], k2, _NT, preferred_element_type=F32)
                         + bias_ref[h + no_prev])
                    m = jnp.max(s, axis=-1, keepdims=True)
                    pv = _dot(jnp.exp(s - m).astype(BF16), v2)
                    parts.append((pv[:, 0:LANES], jnp.broadcast_to(m, (ATTN_SPAN, LANES)), pv[:, LANES:2 * LANES]))
                n_g, m_g, d_g = (jnp.where(first_head, a, b) for a, b in zip(parts[0], parts[1]))
                if g == ATTN_ORDER[0]:
                    na_ref[t, nat, :] = n_g
                    ma_ref[t, nat, :] = m_g
                    da_ref[t, nat, :] = d_g
                else:
                    m_a = ma_ref[t, nat, :]
                    m_n = jnp.maximum(m_a, m_g)
                    e_a = jnp.exp(m_a - m_n)
                    e_g = jnp.exp(m_g - m_n)
                    n_n = na_ref[t, nat, :] * e_a + n_g * e_g
                    d_n = da_ref[t, nat, :] * e_a + d_g * e_g
                    if g == ATTN_ORDER[-1]:
                        o_ref[rows, t * LANES:(t + 1) * LANES] = (n_n / d_n).astype(BF16)
                    else:
                        na_ref[t, nat, :] = n_n
                        da_ref[t, nat, :] = d_n
                        ma_ref[t, nat, :] = m_n
            return carry

        lax.fori_loop(0, SEQ // ATTN_SPAN, block, 0, unroll=ATTN_UNROLL)


def _attn(xb, w, bias, layer):
    n_lt = ATTN_OUT // LANES
    return pl.pallas_call(
        _attn_body,
        grid=(BATCH,),
        in_specs=[pl.BlockSpec((SEQ, D_MODEL), lambda b: (b, 0)),
                  _layer_spec((D_MODEL, 3 * ATTN_WIDTH), layer),
                  pl.BlockSpec((2 * ATTN_HEADS, ATTN_SPAN, 2 * ATTN_SPAN), lambda b: (0, 0, 0),
                               pipeline_mode=pl.Buffered(1))],
        out_specs=pl.BlockSpec((SEQ, ATTN_OUT), lambda b: (b, 0)),
        out_shape=jax.ShapeDtypeStruct((N_TOK, ATTN_OUT), BF16),
        scratch_shapes=[pltpu.VMEM((SEQ // TM * 3 * n_lt, TM, LANES), F32),
                        pltpu.VMEM((ATTN_GROUP_HEADS, SEQ, LANES), BF16),
                        pltpu.VMEM((n_lt, ATTN_SPAN + SEQ, LANES), BF16),
                        pltpu.VMEM((n_lt, ATTN_SPAN + SEQ, 2 * LANES), BF16)]
                       + [pltpu.VMEM((n_lt, SEQ, LANES), F32)] * 3,
        compiler_params=_params("parallel"),
        name="dilated_attn",
    )(xb, w, bias)


def _merge_body(x_ref, ya_ref, yb_ref, yc_ref, wg_ref, gb_ref, pa_ref, pb_ref, pc_ref, wo_ref, g_ref, b_ref, o_ref):
    x = x_ref[...]
    xb = x.astype(BF16)
    merged = None
    for i, (y_ref, p_ref) in enumerate(((ya_ref, pa_ref), (yb_ref, pb_ref), (yc_ref, pc_ref))):
        gate = _sigmoid(_dot(xb, wg_ref[:, i * D_MODEL:(i + 1) * D_MODEL]) + gb_ref[i:i + 1, :])
        term = gate * _dot(y_ref[...], p_ref[...])
        merged = term if merged is None else merged + term
    mix = _dot(merged.astype(BF16), wo_ref[...])
    o_ref[...] = _layer_norm(DN_ALPHA * x + mix, g_ref[...], b_ref[...])


def _merge(x, ya, yb, yc, wg, gb, pa, pb, pc, wo, g, b, layer):
    def row(width):
        return pl.BlockSpec((TM, width), lambda t: (t, 0))

    return pl.pallas_call(
        _merge_body,
        grid=(N_TOK // TM,),
        in_specs=[row(D_MODEL), row(POOL_PAD), row(SSD_INNER), row(ATTN_OUT),
                  _layer_spec((D_MODEL, N_BRANCH * D_MODEL), layer),
                  _layer_spec((N_BRANCH, D_MODEL), layer),
                  _layer_spec((POOL_PAD, D_MODEL), layer),
                  _layer_spec((SSD_INNER, D_MODEL), layer),
                  _layer_spec((ATTN_OUT, D_MODEL), layer),
                  _layer_spec((D_MODEL, D_MODEL), layer),
                  _layer_spec((1, D_MODEL), layer),
                  _layer_spec((1, D_MODEL), layer)],
        out_specs=row(D_MODEL),
        out_shape=jax.ShapeDtypeStruct((N_TOK, D_MODEL), F32),
        compiler_params=_params("parallel"),
        name="gated_merge",
    )(x, ya, yb, yc, wg, gb, pa, pb, pc, wo, g, b)


def _t5_bucket(dist):
    dist = np.maximum(dist, 0)
    max_exact = REL_BUCKETS // 2
    large = max_exact + (np.log(np.maximum(dist, 1) / max_exact) / np.log(REL_MAX_DIST / max_exact)
                         * (REL_BUCKETS - max_exact)).astype(np.int32)
    large = np.minimum(large, REL_BUCKETS - 1)
    return np.where(dist < max_exact, dist, large).astype(np.int32)


def _attn_bias(rel_bias):
    qi = np.arange(ATTN_SPAN)[:, None]
    kj = np.arange(2 * ATTN_SPAN)[None, :]
    delta = qi - kj + ATTN_SPAN
    in_band = (delta >= 0) & (delta <= ATTN_SPAN)
    mats = []
    for gi, (_, dil) in enumerate(ATTN_CONFIGS):
        tab = rel_bias[:, gi * ATTN_GROUP_HEADS:(gi + 1) * ATTN_GROUP_HEADS].astype(F32)
        bucket = jnp.asarray(np.where(in_band, _t5_bucket(delta * dil), -1))
        onehot = (bucket[..., None] == jnp.arange(REL_BUCKETS)).astype(F32)
        b = jnp.einsum('qkb,bh->hqk', onehot, tab, precision=lax.Precision.HIGHEST)
        mats.append(jnp.where(jnp.asarray(in_band), b, NEG_BIG))
    bias = jnp.concatenate(mats, axis=0)
    prev_cols = jnp.asarray(kj < ATTN_SPAN)
    return jnp.concatenate([bias, jnp.where(prev_cols, NEG_BIG, bias)], axis=0)


def _pad_groups(a, axis):
    shp = a.shape
    a = a.reshape(shp[:axis] + (POOL_GROUPS, POOL_GDIM) + shp[axis + 1:])
    pad = [(0, 0)] * a.ndim
    pad[axis + 1] = (0, POOL_GPAD - POOL_GDIM)
    a = jnp.pad(a, pad)
    return a.reshape(shp[:axis] + (POOL_PAD,) + shp[axis + 1:])


def kernel(x, ffn1_w13, ffn1_w2, ln1_g, ln1_b, w_in, gate_b, pool_w, pool_b, pool_scale, conv_w, conv_b, dt_bias, a_log, d_skip, ssd_norm, rel_bias, p_pool, p_ssd, p_attn, w_out, ln2_g, ln2_b, ffn2_w13, ffn2_w2, ln3_g, ln3_b):
    sec = [w_in[:, :, IN_OFFS[i]:IN_OFFS[i + 1]] for i in range(len(IN_SIZES))]
    w_u, w_z, w_xbc, w_dt, w_q, w_k, w_v, w_g = sec
    w_pool = _pad_groups(w_u, 2).astype(BF16)
    w_z = w_z.astype(BF16)
    w_xbc = w_xbc.astype(BF16)
    w_dt = jnp.pad(w_dt, ((0, 0), (0, 0), (0, DT_PAD - SSD_HEADS))).astype(BF16)
    w_qkv = jnp.concatenate(
        [s[:, :, gi * ATTN_OUT:(gi + 1) * ATTN_OUT] for gi in range(len(ATTN_CONFIGS)) for s in (w_q, w_k, w_v)],
        axis=2).astype(BF16)
    w_g = w_g.astype(BF16)

    pw = jnp.pad(pool_w, ((0, 0), (0, 0), (0, POOL_GPAD - POOL_GDIM), (0, POOL_GPAD - POOL_GDIM))).astype(BF16)
    pb = jnp.pad(pool_b, ((0, 0), (0, 0), (0, POOL_GPAD - POOL_GDIM)))
    ps = jnp.pad(pool_scale.reshape(DEPTH, POOL_GROUPS, POOL_GDIM), ((0, 0), (0, 0), (0, POOL_GPAD - POOL_GDIM)))
    p_pool_b = _pad_groups(p_pool, 1).astype(BF16)
    p_ssd_b = p_ssd.astype(BF16)
    p_attn_b = p_attn.astype(BF16)
    w_out_b = w_out.astype(BF16)

    dtb = jnp.pad(dt_bias, ((0, 0), (0, DT_PAD - SSD_HEADS)))[:, None, :]
    alog = jnp.pad(a_log, ((0, 0), (0, DT_PAD - SSD_HEADS)))[:, None, :]
    dsk = jnp.repeat(d_skip, SSD_HEADDIM, axis=1)[:, None, :]
    cb = conv_b[:, None, :]
    nw = ssd_norm[:, None, :]
    bias = _attn_bias(rel_bias)

    f1_w13, f1_w2 = ffn1_w13.astype(BF16), ffn1_w2.astype(BF16)
    f2_w13, f2_w2 = ffn2_w13.astype(BF16), ffn2_w2.astype(BF16)
    vec = lambda a: a[:, None, :]

    h = x.reshape(N_TOK, D_MODEL)
    for i in range(DEPTH):
        h, hb = _ffn(h, f1_w13, f1_w2, vec(ln1_g), vec(ln1_b), i, True)
        ya = _pool(hb, w_pool, pw, pb, ps, i)
        yb = _ssd(hb, w_z, w_xbc, w_dt, conv_w, cb, dtb, alog, dsk, nw, i)
        yc = _attn(hb, w_qkv, bias, i)
        h = _merge(h, ya, yb, yc, w_g, gate_b, p_pool_b, p_ssd_b, p_attn_b, w_out_b, vec(ln2_g), vec(ln2_b), i)
        (h,) = _ffn(h, f2_w13, f2_w2, vec(ln3_g), vec(ln3_b), i, False)
    return h.reshape(BATCH, SEQ, D_MODEL)
```

```python
import functools

import numpy as np
import jax
import jax.numpy as jnp
from jax import lax
from jax.experimental import pallas as pl
from jax.experimental.pallas import tpu as pltpu

F32 = jnp.float32
BF16 = jnp.bfloat16

D_MODEL = 1024
BATCH = 16
SEQ = 2048
DEPTH = 4
N_TOK = BATCH * SEQ
LN_EPS = 1e-5
DN_ALPHA = (2.0 * DEPTH) ** 0.25
FFN_RES = 0.5
D_FF = ((8 * D_MODEL // 3 + 127) // 128) * 128
POOL_WIDTH = 3 * D_MODEL // 4
POOL_WINDOWS = (2, 4, 8, 16)
POOL_GROUPS = len(POOL_WINDOWS)
POOL_GDIM = POOL_WIDTH // POOL_GROUPS
SSD_INNER = D_MODEL
SSD_HEADDIM = 64
SSD_HEADS = SSD_INNER // SSD_HEADDIM
SSD_GROUPS = 4
SSD_HPG = SSD_HEADS // SSD_GROUPS
SSD_STATE = 128
SSD_CONV = 4
SSD_CHUNK = 128
SSD_CONV_CH = SSD_INNER + 2 * SSD_GROUPS * SSD_STATE
SSD_EPS = 1e-5
ATTN_CONFIGS = ((128, 1), (512, 4), (2048, 16))
ATTN_HEAD_DIM = 64
ATTN_GROUP_HEADS = 4
ATTN_HEADS = ATTN_GROUP_HEADS * len(ATTN_CONFIGS)
ATTN_WIDTH = ATTN_HEADS * ATTN_HEAD_DIM
ATTN_OUT = ATTN_GROUP_HEADS * ATTN_HEAD_DIM
ATTN_SPAN = 128
REL_BUCKETS = 32
REL_MAX_DIST = 2048
N_BRANCH = 3
IN_SIZES = (POOL_WIDTH, SSD_INNER, SSD_CONV_CH, SSD_HEADS, ATTN_WIDTH, ATTN_WIDTH, ATTN_WIDTH, N_BRANCH * D_MODEL)
IN_OFFS = tuple(sum(IN_SIZES[:i]) for i in range(len(IN_SIZES) + 1))

LANES = 128
MXU_DIM = 256
VMEM_LIMIT = 56 * 1024 * 1024

TM = 512
FFN_TM = 1024
FFN_CHUNK = MXU_DIM
POOL_GPAD = MXU_DIM
POOL_PAD = POOL_GROUPS * POOL_GPAD
POOL_HALO = 16
POOL_ROWS = 256
DT_PAD = LANES
CONV_HALO = 8
SSD_TS = 512
CONV_STRIDE = 4
CONV_UNIT = 8 * CONV_STRIDE
NEG_BIG = -1e30
ATTN_UNROLL = 8
ATTN_ORDER = (2, 1, 0)
assert ATTN_CONFIGS[ATTN_ORDER[-1]][1] == 1

assert all(w // d == ATTN_SPAN for w, d in ATTN_CONFIGS)
assert POOL_HALO >= max(POOL_WINDOWS) and CONV_HALO >= SSD_CONV - 1

_NT = (((1,), (1,)), ((), ()))
_TN = (((0,), (0,)), ((), ()))


def _dot(a, b):
    return jnp.dot(a, b, preferred_element_type=F32)


def _silu(a):
    h = 0.5 * a
    return h + h * jnp.tanh(h)


def _sigmoid(a):
    return 0.5 + 0.5 * jnp.tanh(0.5 * a)


def _layer_norm(r, g, b):
    mu = jnp.mean(r, axis=-1, keepdims=True)
    c = r - mu
    var = jnp.mean(c * c, axis=-1, keepdims=True)
    return c * lax.rsqrt(var + LN_EPS) * g + b


def _layer_spec(block, layer):
    nd = len(block)
    return pl.BlockSpec((None,) + tuple(block), lambda *_: (layer,) + (0,) * nd,
                        pipeline_mode=pl.Buffered(1))


def _params(*sem):
    return pltpu.CompilerParams(dimension_semantics=sem, vmem_limit_bytes=VMEM_LIMIT)


def _ffn_body(x_ref, w13_ref, w2_ref, g_ref, b_ref, *rest, emit_bf16):
    if emit_bf16:
        o_ref, ob_ref, h_ref = rest
    else:
        o_ref, h_ref = rest
    for half in range(FFN_TM // TM):
        rows = slice(half * TM, (half + 1) * TM)
        x = x_ref[rows, :]
        xb = x.astype(BF16)
        for c in range(D_FF // FFN_CHUNK):
            lo = c * FFN_CHUNK
            a = _dot(xb, w13_ref[:, lo:lo + FFN_CHUNK])
            g = _dot(xb, w13_ref[:, D_FF + lo:D_FF + lo + FFN_CHUNK])
            h_ref[half, :, lo:lo + FFN_CHUNK] = (_silu(a) * g).astype(BF16)
        y = _dot(h_ref[half], w2_ref[...])
        out = _layer_norm(DN_ALPHA * x + FFN_RES * y, g_ref[...], b_ref[...])
        o_ref[rows, :] = out
        if emit_bf16:
            ob_ref[rows, :] = out.astype(BF16)


def _ffn(x, w13, w2, g, b, layer, emit_bf16):
    row = pl.BlockSpec((FFN_TM, D_MODEL), lambda t: (t, 0))
    out_shape = [jax.ShapeDtypeStruct((N_TOK, D_MODEL), F32)]
    out_specs = [row]
    if emit_bf16:
        out_shape.append(jax.ShapeDtypeStruct((N_TOK, D_MODEL), BF16))
        out_specs.append(row)
    return pl.pallas_call(
        functools.partial(_ffn_body, emit_bf16=emit_bf16),
        grid=(N_TOK // FFN_TM,),
        in_specs=[row,
                  _layer_spec((D_MODEL, 2 * D_FF), layer),
                  _layer_spec((D_FF, D_MODEL), layer),
                  _layer_spec((1, D_MODEL), layer),
                  _layer_spec((1, D_MODEL), layer)],
        out_specs=out_specs,
        out_shape=out_shape,
        scratch_shapes=[pltpu.VMEM((FFN_TM // TM, TM, D_FF), BF16)],
        compiler_params=_params("parallel"),
        name="ffn",
    )(x, w13, w2, g, b)


def _pool_body(xb_ref, w_ref, pw_ref, pb_ref, ps_ref, o_ref, u_ref):
    u_ref[0:POOL_HALO, :] = jnp.zeros((POOL_HALO, POOL_PAD), F32)

    def proj(j):
        u_ref[POOL_HALO + j * TM:POOL_HALO + (j + 1) * TM, :] = _dot(xb_ref[j * TM:(j + 1) * TM, :], w_ref[...])

    def chunk(c):
        r0 = c * POOL_ROWS
        t = r0 + lax.broadcasted_iota(jnp.int32, (POOL_ROWS, 1), 0)
        for g, win in enumerate(POOL_WINDOWS):
            cols = slice(g * POOL_GPAD, (g + 1) * POOL_GPAD)
            ug = u_ref[r0:r0 + POOL_ROWS + POOL_HALO, cols]
            s = ug + pltpu.roll(ug, 1, 0)
            k = 2
            while k < win:
                s = s + pltpu.roll(s, k, 0)
                k *= 2
            inv = 1.0 / jnp.minimum(t + 1, win).astype(F32)
            pooled = s[POOL_HALO:] * inv - ug[POOL_HALO:]
            y = _dot(pooled.astype(BF16), pw_ref[g])
            y = (y + pb_ref[g:g + 1, :]) * ps_ref[g:g + 1, :]
            o_ref[r0:r0 + POOL_ROWS, cols] = y.astype(BF16)

    per_tile = TM // POOL_ROWS
    for j in range(SEQ // TM + 1):
        if j < SEQ // TM:
            proj(j)
        if j > 0:
            for c in range((j - 1) * per_tile, j * per_tile):
                chunk(c)


def _pool(xb, w, pw, pb, ps, layer):
    seq = pl.BlockSpec((SEQ, D_MODEL), lambda b: (b, 0))
    return pl.pallas_call(
        _pool_body,
        grid=(BATCH,),
        in_specs=[seq,
                  _layer_spec((D_MODEL, POOL_PAD), layer),
                  _layer_spec((POOL_GROUPS, POOL_GPAD, POOL_GPAD), layer),
                  _layer_spec((POOL_GROUPS, POOL_GPAD), layer),
                  _layer_spec((POOL_GROUPS, POOL_GPAD), layer)],
        out_specs=pl.BlockSpec((SEQ, POOL_PAD), lambda b: (b, 0)),
        out_shape=jax.ShapeDtypeStruct((N_TOK, POOL_PAD), BF16),
        scratch_shapes=[pltpu.VMEM((POOL_HALO + SEQ, POOL_PAD), F32)],
        compiler_params=_params("parallel"),
        name="pool_mixer",
    )(xb, w, pw, pb, ps)


def _split3_bf16(v):
    hi = v.astype(BF16)
    r1 = v - hi.astype(F32)
    mid = r1.astype(BF16)
    lo = (r1 - mid.astype(F32)).astype(BF16)
    return hi, mid, lo


def _softplus(v):
    return jnp.maximum(v, 0.0) + jnp.log1p(jnp.exp(-jnp.abs(v)))


def _interleaved_time(i):
    return (i & ~(CONV_UNIT - 1)) | ((i & 7) << 2) | ((i >> 3) & (CONV_STRIDE - 1))


def _ssd_body(xb_ref, wz_ref, wx_ref, wdt_ref, ex_ref, cw_ref, cb_ref, dtb_ref, alog_ref, dsk_ref, nw_ref,
              o_ref, pad_ref, xc_ref, dt_ref, acum_ref, ex2_ref, y_ref, st_ref):
    n_ct = SSD_CONV_CH // LANES
    n_chunks = SSD_TS // SSD_CHUNK
    units = [(u, v) for u in range(SSD_CHUNK // CONV_UNIT) for v in range(CONV_STRIDE)]

    @pl.when(pl.program_id(1) == 0)
    def _():
        st_ref[...] = jnp.zeros(st_ref.shape, F32)
        for c in range(n_ct):
            pad_ref[c, 0:CONV_HALO, :] = jnp.zeros((CONV_HALO, LANES), F32)

    def interleaved(ref_rows, base):
        return jnp.concatenate([ref_rows(pl.ds(base + u * CONV_UNIT + v, 8, stride=CONV_STRIDE))
                                for u, v in units], axis=0)

    xb = xb_ref[...]
    raw = _dot(xb, wx_ref[...])
    for c in range(n_ct):
        pad_ref[c, CONV_HALO:CONV_HALO + SSD_TS, :] = raw[:, c * LANES:(c + 1) * LANES]

    def conv(k):
        for c in range(n_ct):
            cols = slice(c * LANES, (c + 1) * LANES)
            acc = None
            for j in range(SSD_CONV):
                tap = jnp.broadcast_to(cw_ref[j:j + 1, cols], (SSD_CHUNK, LANES))
                base = CONV_HALO + k * SSD_CHUNK - (SSD_CONV - 1) + j
                term = tap * interleaved(lambda rows, c=c: pad_ref[c, rows, :], base)
                acc = term if acc is None else acc + term
            xc_ref[pl.ds(k * SSD_CHUNK, SSD_CHUNK), cols] = _silu(acc + cb_ref[:, cols])

    for k in range(n_chunks):
        conv(k)
    for c in range(n_ct):
        pad_ref[c, 0:CONV_HALO, :] = pad_ref[c, SSD_TS:SSD_TS + CONV_HALO, :]

    li = _interleaved_time(lax.broadcasted_iota(jnp.int32, (SSD_CHUNK, SSD_CHUNK), 0))
    si = _interleaved_time(lax.broadcasted_iota(jnp.int32, (SSD_CHUNK, SSD_CHUNK), 1))
    tril = li >= si
    ones_tril = jnp.where(tril, 1.0, 0.0).astype(BF16)
    first_head = lax.broadcasted_iota(jnp.int32, (SSD_CHUNK, LANES), 1) < SSD_HEADDIM
    gw = SSD_HPG * SSD_HEADDIM

    dt_ref[...] = _softplus(_dot(xb, wdt_ref[...]) + dtb_ref[...])
    a_neg = -jnp.exp(alog_ref[...])
    dts = []
    for k in range(n_chunks):
        dt_k = interleaved(lambda rows: dt_ref[rows, :], k * SSD_CHUNK)
        parts = _dot(ones_tril, jnp.concatenate(_split3_bf16(dt_k * a_neg), axis=1))
        acum_ref[pl.ds(k * SSD_CHUNK, SSD_CHUNK), :] = (
            (parts[:, 0:DT_PAD] + parts[:, DT_PAD:2 * DT_PAD]) + parts[:, 2 * DT_PAD:3 * DT_PAD])
        dts.append(dt_k)
    both = jnp.concatenate(dts + [acum_ref[...]], axis=0)
    ex2_ref[...] = _dot(jnp.concatenate(_split3_bf16(both), axis=1), ex_ref[...])

    def chunk(k):
        rows = pl.ds(k * SSD_CHUNK, SSD_CHUNK)
        acum = acum_ref[rows, :]
        acum_t = acum.T
        acx = ex2_ref[pl.ds(SSD_TS + k * SSD_CHUNK, SSD_CHUNK), :]
        a_last = acx[SSD_CHUNK - 1:SSD_CHUNK, :]
        xs = xc_ref[rows, 0:SSD_INNER]
        xdt = xs * ex2_ref[rows, :]
        xw_b = (xdt * jnp.exp(a_last - acx)).astype(BF16)
        e_cum = jnp.exp(acx)
        e_last = jnp.exp(a_last)
        for g in range(SSD_GROUPS):
            gc = slice(g * gw, (g + 1) * gw)
            b_lo = SSD_INNER + g * SSD_STATE
            c_lo = SSD_INNER + SSD_GROUPS * SSD_STATE + g * SSD_STATE
            bmb = xc_ref[rows, b_lo:b_lo + SSD_STATE].astype(BF16)
            cmb = xc_ref[rows, c_lo:c_lo + SSD_STATE].astype(BF16)
            cbm = lax.dot_general(cmb, bmb, _NT, preferred_element_type=F32)
            prev = st_ref[g]
            y_off = _dot(cmb, prev.astype(BF16)) * e_cum[:, gc]
            new = lax.dot_general(bmb, xw_b[:, gc], _TN, preferred_element_type=F32)
            st_ref[g] = prev * e_last[:, gc] + new
            for t in range(gw // LANES):
                tc = slice(g * gw + t * LANES, g * gw + (t + 1) * LANES)
                x_pair = xdt[:, tc]
                mix = []
                for e in range(LANES // SSD_HEADDIM):
                    h = g * SSD_HPG + t * (LANES // SSD_HEADDIM) + e
                    diff = acum[:, h:h + 1] - acum_t[h:h + 1, :]
                    decay = jnp.exp(jnp.where(tril, diff, -jnp.inf))
                    mix.append((cbm * decay).astype(BF16))
                x_split = jnp.concatenate([jnp.where(first_head, x_pair, 0.0).astype(BF16),
                                           jnp.where(first_head, 0.0, x_pair).astype(BF16)], axis=0)
                y_t = (_dot(jnp.concatenate(mix, axis=1), x_split) + y_off[:, t * LANES:(t + 1) * LANES]
                       + dsk_ref[:, tc] * xs[:, tc])
                for n, (u, v) in enumerate(units):
                    y_ref[g * (gw // LANES) + t,
                          pl.ds(k * SSD_CHUNK + u * CONV_UNIT + v, 8, stride=CONV_STRIDE), :] = y_t[n * 8:(n + 1) * 8, :]

    for k in range(n_chunks):
        chunk(k)

    z = _dot(xb, wz_ref[...])
    for g in range(SSD_GROUPS):
        cols = slice(g * gw, (g + 1) * gw)
        yg = jnp.concatenate([y_ref[g * (gw // LANES) + t] for t in range(gw // LANES)], axis=1)
        yg = yg * _silu(z[:, cols])
        ms = jnp.mean(yg * yg, axis=-1, keepdims=True)
        o_ref[:, cols] = (yg * lax.rsqrt(ms + SSD_EPS) * nw_ref[:, cols]).astype(BF16)


def _head_expander():
    e = (np.arange(SSD_INNER)[None, :] // SSD_HEADDIM == np.arange(DT_PAD)[:, None]).astype(np.float32)
    return jnp.asarray(np.concatenate([e, e, e], axis=0), BF16)


def _ssd(xb, wz, wx, wdt, cw, cb, dtb, alog, dsk, nw, layer):
    steps = SEQ // SSD_TS
    row = pl.BlockSpec((SSD_TS, D_MODEL), lambda b, s: (b * steps + s, 0))
    return pl.pallas_call(
        _ssd_body,
        grid=(BATCH, steps),
        in_specs=[row,
                  _layer_spec((D_MODEL, SSD_INNER), layer),
                  _layer_spec((D_MODEL, SSD_CONV_CH), layer),
                  _layer_spec((D_MODEL, DT_PAD), layer),
                  pl.BlockSpec((3 * DT_PAD, SSD_INNER), lambda b, s: (0, 0), pipeline_mode=pl.Buffered(1)),
                  _layer_spec((SSD_CONV, SSD_CONV_CH), layer),
                  _layer_spec((1, SSD_CONV_CH), layer),
                  _layer_spec((1, DT_PAD), layer),
                  _layer_spec((1, DT_PAD), layer),
                  _layer_spec((1, SSD_INNER), layer),
                  _layer_spec((1, SSD_INNER), layer)],
        out_specs=pl.BlockSpec((SSD_TS, SSD_INNER), lambda b, s: (b * steps + s, 0)),
        out_shape=jax.ShapeDtypeStruct((N_TOK, SSD_INNER), BF16),
        scratch_shapes=[pltpu.VMEM((SSD_CONV_CH // LANES, CONV_HALO + SSD_TS, LANES), F32),
                        pltpu.VMEM((SSD_TS, SSD_CONV_CH), F32),
                        pltpu.VMEM((SSD_TS, DT_PAD), F32),
                        pltpu.VMEM((SSD_TS, DT_PAD), F32),
                        pltpu.VMEM((2 * SSD_TS, SSD_INNER), F32),
                        pltpu.VMEM((SSD_INNER // LANES, SSD_TS, LANES), F32),
                        pltpu.VMEM((SSD_GROUPS, SSD_STATE, SSD_HPG * SSD_HEADDIM), F32)],
        compiler_params=_params("arbitrary", "arbitrary"),
        name="ssd_mixer",
    )(xb, wz, wx, wdt, _head_expander(), cw, cb, dtb, alog, dsk, nw)


def _attn_body(xb_ref, w_ref, bias_ref, o_ref, tile_ref, qh_ref, kb_ref, vb_ref, na_ref, ma_ref, da_ref):
    n_lt = ATTN_OUT // LANES
    hpt = LANES // ATTN_HEAD_DIM
    lane = lax.broadcasted_iota(jnp.int32, (ATTN_SPAN, LANES), 1)
    first_head = lane < ATTN_HEAD_DIM
    for t in range(n_lt):
        kb_ref[t, 0:ATTN_SPAN, :] = jnp.zeros((ATTN_SPAN, LANES), BF16)
        vb_ref[t, 0:ATTN_SPAN, 0:LANES] = jnp.zeros((ATTN_SPAN, LANES), BF16)
        vb_ref[t, :, LANES:2 * LANES] = jnp.ones((ATTN_SPAN + SEQ, LANES), BF16)

    for g in ATTN_ORDER:
        dil = ATTN_CONFIGS[g][1]
        sub = SEQ // dil
        n_blk = sub // ATTN_SPAN
        per = TM // dil
        wcol = g * 3 * ATTN_OUT
        first_rows = lax.broadcasted_iota(jnp.int32, (per, LANES), 1) < ATTN_HEAD_DIM

        for j in range(SEQ // TM):
            qkv = _dot(xb_ref[j * TM:(j + 1) * TM, :], w_ref[:, wcol:wcol + 3 * ATTN_OUT])
            for which in range(3):
                for t in range(n_lt):
                    c0 = which * ATTN_OUT + t * LANES
                    piece = qkv[:, c0:c0 + LANES]
                    if which == 0:
                        piece = piece * (ATTN_HEAD_DIM ** -0.5)
                    slot = (j * 3 + which) * n_lt + t
                    if dil > 1:
                        tile_ref[slot] = piece
                    for r in range(dil):
                        if dil > 1:
                            piece = tile_ref[slot, pl.ds(r, per, stride=dil), :]
                        if which == 0:
                            dst = pl.ds(r * sub + j * per, per)
                            qh_ref[t * hpt, dst, :] = jnp.where(first_rows, piece, 0.0).astype(BF16)
                            qh_ref[t * hpt + 1, dst, :] = jnp.where(first_rows, 0.0, piece).astype(BF16)
                        elif which == 1:
                            kb_ref[t, pl.ds(ATTN_SPAN + r * sub + j * per, per), :] = piece.astype(BF16)
                        else:
                            vb_ref[t, pl.ds(ATTN_SPAN + r * sub + j * per, per), 0:LANES] = piece.astype(BF16)

        def block(u, carry, g=g, n_blk=n_blk, dil=dil):
            r0 = pl.multiple_of(u * ATTN_SPAN, ATTN_SPAN)
            rows = pl.ds(r0, ATTN_SPAN)
            krows = pl.ds(r0, 2 * ATTN_SPAN)
            no_prev = ATTN_HEADS if n_blk == 1 else jnp.where(u % n_blk == 0, ATTN_HEADS, 0)
            nat = rows if dil == 1 else pl.ds(u // n_blk + (u % n_blk) * (ATTN_SPAN * dil), ATTN_SPAN, stride=dil)
            for t in range(n_lt):
                k2 = kb_ref[t, krows, :]
                v2 = vb_ref[t, krows, :]
                parts = []
                for e in range(hpt):
                    h = g * ATTN_GROUP_HEADS + t * hpt + e
                    s = (lax.dot_general(qh_ref[t * hpt + e, rows, :], k2, _NT, preferred_element_type=F32)
                         + bias_ref[h + no_prev])
                    m = jnp.max(s, axis=-1, keepdims=True)
                    pv = _dot(jnp.exp(s - m).astype(BF16), v2)
                    parts.append((pv[:, 0:LANES], jnp.broadcast_to(m, (ATTN_SPAN, LANES)), pv[:, LANES:2 * LANES]))
                n_g, m_g, d_g = (jnp.where(first_head, a, b) for a, b in zip(parts[0], parts[1]))
                if g == ATTN_ORDER[0]:
                    na_ref[t, nat, :] = n_g
                    ma_ref[t, nat, :] = m_g
                    da_ref[t, nat, :] = d_g
                else:
                    m_a = ma_ref[t, nat, :]
                    m_n = jnp.maximum(m_a, m_g)
                    e_a = jnp.exp(m_a - m_n)
                    e_g = jnp.exp(m_g - m_n)
                    n_n = na_ref[t, nat, :] * e_a + n_g * e_g
                    d_n = da_ref[t, nat, :] * e_a + d_g * e_g
                    if g == ATTN_ORDER[-1]:
                        o_ref[rows, t * LANES:(t + 1) * LANES] = (n_n / d_n).astype(BF16)
                    else:
                        na_ref[t, nat, :] = n_n
                        da_ref[t, nat, :] = d_n
                        ma_ref[t, nat, :] = m_n
            return carry

        lax.fori_loop(0, SEQ // ATTN_SPAN, block, 0, unroll=ATTN_UNROLL)


def _attn(xb, w, bias, layer):
    n_lt = ATTN_OUT // LANES
    return pl.pallas_call(
        _attn_body,
        grid=(BATCH,),
        in_specs=[pl.BlockSpec((SEQ, D_MODEL), lambda b: (b, 0)),
                  _layer_spec((D_MODEL, 3 * ATTN_WIDTH), layer),
                  pl.BlockSpec((2 * ATTN_HEADS, ATTN_SPAN, 2 * ATTN_SPAN), lambda b: (0, 0, 0),
                               pipeline_mode=pl.Buffered(1))],
        out_specs=pl.BlockSpec((SEQ, ATTN_OUT), lambda b: (b, 0)),
        out_shape=jax.ShapeDtypeStruct((N_TOK, ATTN_OUT), BF16),
        scratch_shapes=[pltpu.VMEM((SEQ // TM * 3 * n_lt, TM, LANES), F32),
                        pltpu.VMEM((ATTN_GROUP_HEADS, SEQ, LANES), BF16),
                        pltpu.VMEM((n_lt, ATTN_SPAN + SEQ, LANES), BF16),
                        pltpu.VMEM((n_lt, ATTN_SPAN + SEQ, 2 * LANES), BF16)]
                       + [pltpu.VMEM((n_lt, SEQ, LANES), F32)] * 3,
        compiler_params=_params("parallel"),
        name="dilated_attn",
    )(xb, w, bias)


def _merge_body(x_ref, ya_ref, yb_ref, yc_ref, wg_ref, gb_ref, pa_ref, pb_ref, pc_ref, wo_ref, g_ref, b_ref, o_ref,
                m_ref):
    x = x_ref[...]
    xb = x.astype(BF16)
    for c in range(D_MODEL // MXU_DIM):
        cs = slice(c * MXU_DIM, (c + 1) * MXU_DIM)
        acc = None
        for i, (y_ref, p_ref) in enumerate(((ya_ref, pa_ref), (yb_ref, pb_ref), (yc_ref, pc_ref))):
            gs = slice(i * D_MODEL + c * MXU_DIM, i * D_MODEL + (c + 1) * MXU_DIM)
            gate = _sigmoid(_dot(xb, wg_ref[:, gs]) + gb_ref[i:i + 1, cs])
            term = gate * _dot(y_ref[...], p_ref[:, cs])
            acc = term if acc is None else acc + term
        m_ref[:, cs] = acc.astype(BF16)
    mix = _dot(m_ref[...], wo_ref[...])
    o_ref[...] = _layer_norm(DN_ALPHA * x + mix, g_ref[...], b_ref[...])


def _merge(x, ya, yb, yc, wg, gb, pa, pb, pc, wo, g, b, layer):
    def row(width):
        return pl.BlockSpec((TM, width), lambda t: (t, 0))

    return pl.pallas_call(
        _merge_body,
        grid=(N_TOK // TM,),
        in_specs=[row(D_MODEL), row(POOL_PAD), row(SSD_INNER), row(ATTN_OUT),
                  _layer_spec((D_MODEL, N_BRANCH * D_MODEL), layer),
                  _layer_spec((N_BRANCH, D_MODEL), layer),
                  _layer_spec((POOL_PAD, D_MODEL), layer),
                  _layer_spec((SSD_INNER, D_MODEL), layer),
                  _layer_spec((ATTN_OUT, D_MODEL), layer),
                  _layer_spec((D_MODEL, D_MODEL), layer),
                  _layer_spec((1, D_MODEL), layer),
                  _layer_spec((1, D_MODEL), layer)],
        out_specs=row(D_MODEL),
        out_shape=jax.ShapeDtypeStruct((N_TOK, D_MODEL), F32),
        scratch_shapes=[pltpu.VMEM((TM, D_MODEL), BF16)],
        compiler_params=_params("parallel"),
        name="gated_merge",
    )(x, ya, yb, yc, wg, gb, pa, pb, pc, wo, g, b)


def _t5_bucket(dist):
    dist = np.maximum(dist, 0)
    max_exact = REL_BUCKETS // 2
    large = max_exact + (np.log(np.maximum(dist, 1) / max_exact) / np.log(REL_MAX_DIST / max_exact)
                         * (REL_BUCKETS - max_exact)).astype(np.int32)
    large = np.minimum(large, REL_BUCKETS - 1)
    return np.where(dist < max_exact, dist, large).astype(np.int32)


def _attn_bias(rel_bias):
    qi = np.arange(ATTN_SPAN)[:, None]
    kj = np.arange(2 * ATTN_SPAN)[None, :]
    delta = qi - kj + ATTN_SPAN
    in_band = (delta >= 0) & (delta <= ATTN_SPAN)
    mats = []
    for gi, (_, dil) in enumerate(ATTN_CONFIGS):
        tab = rel_bias[:, gi * ATTN_GROUP_HEADS:(gi + 1) * ATTN_GROUP_HEADS].astype(F32)
        bucket = jnp.asarray(np.where(in_band, _t5_bucket(delta * dil), -1))
        onehot = (bucket[..., None] == jnp.arange(REL_BUCKETS)).astype(F32)
        b = jnp.einsum('qkb,bh->hqk', onehot, tab, precision=lax.Precision.HIGHEST)
        mats.append(jnp.where(jnp.asarray(in_band), b, NEG_BIG))
    bias = jnp.concatenate(mats, axis=0)
    prev_cols = jnp.asarray(kj < ATTN_SPAN)
    return jnp.concatenate([bias, jnp.where(prev_cols, NEG_BIG, bias)], axis=0)


def _pad_groups(a, axis):
    shp = a.shape
    a = a.reshape(shp[:axis] + (POOL_GROUPS, POOL_GDIM) + shp[axis + 1:])
    pad = [(0, 0)] * a.ndim
    pad[axis + 1] = (0, POOL_GPAD - POOL_GDIM)
    a = jnp.pad(a, pad)
    return a.reshape(shp[:axis] + (POOL_PAD,) + shp[axis + 1:])


def kernel(x, ffn1_w13, ffn1_w2, ln1_g, ln1_b, w_in, gate_b, pool_w, pool_b, pool_scale, conv_w, conv_b, dt_bias, a_log, d_skip, ssd_norm, rel_bias, p_pool, p_ssd, p_attn, w_out, ln2_g, ln2_b, ffn2_w13, ffn2_w2, ln3_g, ln3_b):
    sec = [w_in[:, :, IN_OFFS[i]:IN_OFFS[i + 1]] for i in range(len(IN_SIZES))]
    w_u, w_z, w_xbc, w_dt, w_q, w_k, w_v, w_g = sec
    w_pool = _pad_groups(w_u, 2).astype(BF16)
    w_z = w_z.astype(BF16)
    w_xbc = w_xbc.astype(BF16)
    w_dt = jnp.pad(w_dt, ((0, 0), (0, 0), (0, DT_PAD - SSD_HEADS))).astype(BF16)
    w_qkv = jnp.concatenate(
        [s[:, :, gi * ATTN_OUT:(gi + 1) * ATTN_OUT] for gi in range(len(ATTN_CONFIGS)) for s in (w_q, w_k, w_v)],
        axis=2).astype(BF16)
    w_g = w_g.astype(BF16)

    pw = jnp.pad(pool_w, ((0, 0), (0, 0), (0, POOL_GPAD - POOL_GDIM), (0, POOL_GPAD - POOL_GDIM))).astype(BF16)
    pb = jnp.pad(pool_b, ((0, 0), (0, 0), (0, POOL_GPAD - POOL_GDIM)))
    ps = jnp.pad(pool_scale.reshape(DEPTH, POOL_GROUPS, POOL_GDIM), ((0, 0), (0, 0), (0, POOL_GPAD - POOL_GDIM)))
    p_pool_b = _pad_groups(p_pool, 1).astype(BF16)
    p_ssd_b = p_ssd.astype(BF16)
    p_attn_b = p_attn.astype(BF16)
    w_out_b = w_out.astype(BF16)

    dtb = jnp.pad(dt_bias, ((0, 0), (0, DT_PAD - SSD_HEADS)))[:, None, :]
    alog = jnp.pad(a_log, ((0, 0), (0, DT_PAD - SSD_HEADS)))[:, None, :]
    dsk = jnp.repeat(d_skip, SSD_HEADDIM, axis=1)[:, None, :]
    cb = conv_b[:, None, :]
    nw = ssd_norm[:, None, :]
    bias = _attn_bias(rel_bias)

    f1_w13, f1_w2 = ffn1_w13.astype(BF16), ffn1_w2.astype(BF16)
    f2_w13, f2_w2 = ffn2_w13.astype(BF16), ffn2_w2.astype(BF16)
    vec = lambda a: a[:, None, :]

    h = x.reshape(N_TOK, D_MODEL)
    for i in range(DEPTH):
        h, hb = _ffn(h, f1_w13, f1_w2, vec(ln1_g), vec(ln1_b), i, True)
        ya = _pool(hb, w_pool, pw, pb, ps, i)
        yb = _ssd(hb, w_z, w_xbc, w_dt, conv_w, cb, dtb, alog, dsk, nw, i)
        yc = _attn(hb, w_qkv, bias, i)
        h = _merge(h, ya, yb, yc, w_g, gate_b, p_pool_b, p_ssd_b, p_attn_b, w_out_b, vec(ln2_g), vec(ln2_b), i)
        (h,) = _ffn(h, f2_w13, f2_w2, vec(ln3_g), vec(ln3_b), i, False)
    return h.reshape(BATCH, SEQ, D_MODEL)
```

```python
import functools

import numpy as np
import jax
import jax.numpy as jnp
from jax import lax
from jax.experimental import pallas as pl
from jax.experimental.pallas import tpu as pltpu

F32 = jnp.float32
BF16 = jnp.bfloat16

D_MODEL = 1024
BATCH = 16
SEQ = 2048
DEPTH = 4
N_TOK = BATCH * SEQ
LN_EPS = 1e-5
DN_ALPHA = (2.0 * DEPTH) ** 0.25
FFN_RES = 0.5
D_FF = ((8 * D_MODEL // 3 + 127) // 128) * 128
POOL_WIDTH = 3 * D_MODEL // 4
POOL_WINDOWS = (2, 4, 8, 16)
POOL_GROUPS = len(POOL_WINDOWS)
POOL_GDIM = POOL_WIDTH // POOL_GROUPS
SSD_INNER = D_MODEL
SSD_HEADDIM = 64
SSD_HEADS = SSD_INNER // SSD_HEADDIM
SSD_GROUPS = 4
SSD_HPG = SSD_HEADS // SSD_GROUPS
SSD_STATE = 128
SSD_CONV = 4
SSD_CHUNK = 128
SSD_CONV_CH = SSD_INNER + 2 * SSD_GROUPS * SSD_STATE
SSD_EPS = 1e-5
ATTN_CONFIGS = ((128, 1), (512, 4), (2048, 16))
ATTN_HEAD_DIM = 64
ATTN_GROUP_HEADS = 4
ATTN_HEADS = ATTN_GROUP_HEADS * len(ATTN_CONFIGS)
ATTN_WIDTH = ATTN_HEADS * ATTN_HEAD_DIM
ATTN_OUT = ATTN_GROUP_HEADS * ATTN_HEAD_DIM
ATTN_SPAN = 128
REL_BUCKETS = 32
REL_MAX_DIST = 2048
N_BRANCH = 3
IN_SIZES = (POOL_WIDTH, SSD_INNER, SSD_CONV_CH, SSD_HEADS, ATTN_WIDTH, ATTN_WIDTH, ATTN_WIDTH, N_BRANCH * D_MODEL)
IN_OFFS = tuple(sum(IN_SIZES[:i]) for i in range(len(IN_SIZES) + 1))

LANES = 128
MXU_DIM = 256
VMEM_LIMIT = 56 * 1024 * 1024

TM = 512
FFN_TM = 1024
FFN_CHUNK = MXU_DIM
POOL_GPAD = MXU_DIM
POOL_PAD = POOL_GROUPS * POOL_GPAD
POOL_HALO = 16
POOL_ROWS = 256
DT_PAD = LANES
CONV_HALO = 8
SSD_TS = 512
CONV_STRIDE = 4
CONV_UNIT = 8 * CONV_STRIDE
NEG_BIG = -1e30
ATTN_UNROLL = 16
ATTN_ORDER = (2, 1, 0)
assert ATTN_CONFIGS[ATTN_ORDER[-1]][1] == 1

assert all(w // d == ATTN_SPAN for w, d in ATTN_CONFIGS)
assert POOL_HALO >= max(POOL_WINDOWS) and CONV_HALO >= SSD_CONV - 1

_NT = (((1,), (1,)), ((), ()))
_TN = (((0,), (0,)), ((), ()))


def _dot(a, b):
    return jnp.dot(a, b, preferred_element_type=F32)


def _silu(a):
    h = 0.5 * a
    return h + h * jnp.tanh(h)


def _sigmoid(a):
    return 0.5 + 0.5 * jnp.tanh(0.5 * a)


def _layer_norm(r, g, b):
    mu = jnp.mean(r, axis=-1, keepdims=True)
    c = r - mu
    var = jnp.mean(c * c, axis=-1, keepdims=True)
    return c * lax.rsqrt(var + LN_EPS) * g + b


def _layer_spec(block, layer):
    nd = len(block)
    return pl.BlockSpec((None,) + tuple(block), lambda *_: (layer,) + (0,) * nd,
                        pipeline_mode=pl.Buffered(1))


def _params(*sem):
    return pltpu.CompilerParams(dimension_semantics=sem, vmem_limit_bytes=VMEM_LIMIT)


def _ffn_body(x_ref, w13_ref, w2_ref, g_ref, b_ref, *rest, emit_bf16):
    if emit_bf16:
        o_ref, ob_ref, h_ref = rest
    else:
        o_ref, h_ref = rest
    for half in range(FFN_TM // TM):
        rows = slice(half * TM, (half + 1) * TM)
        x = x_ref[rows, :]
        xb = x.astype(BF16)
        for c in range(D_FF // FFN_CHUNK):
            lo = c * FFN_CHUNK
            a = _dot(xb, w13_ref[:, lo:lo + FFN_CHUNK])
            g = _dot(xb, w13_ref[:, D_FF + lo:D_FF + lo + FFN_CHUNK])
            h_ref[half, :, lo:lo + FFN_CHUNK] = (_silu(a) * g).astype(BF16)
        y = _dot(h_ref[half], w2_ref[...])
        out = _layer_norm(DN_ALPHA * x + FFN_RES * y, g_ref[...], b_ref[...])
        o_ref[rows, :] = out
        if emit_bf16:
            ob_ref[rows, :] = out.astype(BF16)


def _ffn(x, w13, w2, g, b, layer, emit_bf16):
    row = pl.BlockSpec((FFN_TM, D_MODEL), lambda t: (t, 0))
    out_shape = [jax.ShapeDtypeStruct((N_TOK, D_MODEL), F32)]
    out_specs = [row]
    if emit_bf16:
        out_shape.append(jax.ShapeDtypeStruct((N_TOK, D_MODEL), BF16))
        out_specs.append(row)
    return pl.pallas_call(
        functools.partial(_ffn_body, emit_bf16=emit_bf16),
        grid=(N_TOK // FFN_TM,),
        in_specs=[row,
                  _layer_spec((D_MODEL, 2 * D_FF), layer),
                  _layer_spec((D_FF, D_MODEL), layer),
                  _layer_spec((1, D_MODEL), layer),
                  _layer_spec((1, D_MODEL), layer)],
        out_specs=out_specs,
        out_shape=out_shape,
        scratch_shapes=[pltpu.VMEM((FFN_TM // TM, TM, D_FF), BF16)],
        compiler_params=_params("parallel"),
        name="ffn",
    )(x, w13, w2, g, b)


def _pool_body(xb_ref, w_ref, pw_ref, pb_ref, ps_ref, o_ref, u_ref):
    u_ref[0:POOL_HALO, :] = jnp.zeros((POOL_HALO, POOL_PAD), F32)

    def proj(j):
        u_ref[POOL_HALO + j * TM:POOL_HALO + (j + 1) * TM, :] = _dot(xb_ref[j * TM:(j + 1) * TM, :], w_ref[...])

    def chunk(c):
        r0 = c * POOL_ROWS
        t = r0 + lax.broadcasted_iota(jnp.int32, (POOL_ROWS, 1), 0)
        for g, win in enumerate(POOL_WINDOWS):
            cols = slice(g * POOL_GPAD, (g + 1) * POOL_GPAD)
            ug = u_ref[r0:r0 + POOL_ROWS + POOL_HALO, cols]
            s = ug + pltpu.roll(ug, 1, 0)
            k = 2
            while k < win:
                s = s + pltpu.roll(s, k, 0)
                k *= 2
            inv = 1.0 / jnp.minimum(t + 1, win).astype(F32)
            pooled = s[POOL_HALO:] * inv - ug[POOL_HALO:]
            y = _dot(pooled.astype(BF16), pw_ref[g])
            y = (y + pb_ref[g:g + 1, :]) * ps_ref[g:g + 1, :]
            o_ref[r0:r0 + POOL_ROWS, cols] = y.astype(BF16)

    per_tile = TM // POOL_ROWS
    for j in range(SEQ // TM + 1):
        if j < SEQ // TM:
            proj(j)
        if j > 0:
            for c in range((j - 1) * per_tile, j * per_tile):
                chunk(c)


def _pool(xb, w, pw, pb, ps, layer):
    seq = pl.BlockSpec((SEQ, D_MODEL), lambda b: (b, 0))
    return pl.pallas_call(
        _pool_body,
        grid=(BATCH,),
        in_specs=[seq,
                  _layer_spec((D_MODEL, POOL_PAD), layer),
                  _layer_spec((POOL_GROUPS, POOL_GPAD, POOL_GPAD), layer),
                  _layer_spec((POOL_GROUPS, POOL_GPAD), layer),
                  _layer_spec((POOL_GROUPS, POOL_GPAD), layer)],
        out_specs=pl.BlockSpec((SEQ, POOL_PAD), lambda b: (b, 0)),
        out_shape=jax.ShapeDtypeStruct((N_TOK, POOL_PAD), BF16),
        scratch_shapes=[pltpu.VMEM((POOL_HALO + SEQ, POOL_PAD), F32)],
        compiler_params=_params("parallel"),
        name="pool_mixer",
    )(xb, w, pw, pb, ps)


def _split3_bf16(v):
    hi = v.astype(BF16)
    r1 = v - hi.astype(F32)
    mid = r1.astype(BF16)
    lo = (r1 - mid.astype(F32)).astype(BF16)
    return hi, mid, lo


def _softplus(v):
    return jnp.maximum(v, 0.0) + jnp.log1p(jnp.exp(-jnp.abs(v)))


def _interleaved_time(i):
    return (i & ~(CONV_UNIT - 1)) | ((i & 7) << 2) | ((i >> 3) & (CONV_STRIDE - 1))


def _ssd_body(xb_ref, wz_ref, wx_ref, wdt_ref, ex_ref, cw_ref, cb_ref, dtb_ref, alog_ref, dsk_ref, nw_ref,
              o_ref, pad_ref, xc_ref, dt_ref, acum_ref, ex2_ref, y_ref, st_ref):
    n_ct = SSD_CONV_CH // LANES
    n_chunks = SSD_TS // SSD_CHUNK
    units = [(u, v) for u in range(SSD_CHUNK // CONV_UNIT) for v in range(CONV_STRIDE)]

    @pl.when(pl.program_id(1) == 0)
    def _():
        st_ref[...] = jnp.zeros(st_ref.shape, F32)
        for c in range(n_ct):
            pad_ref[c, 0:CONV_HALO, :] = jnp.zeros((CONV_HALO, LANES), F32)

    def interleaved(ref_rows, base):
        return jnp.concatenate([ref_rows(pl.ds(base + u * CONV_UNIT + v, 8, stride=CONV_STRIDE))
                                for u, v in units], axis=0)

    xb = xb_ref[...]
    raw = _dot(xb, wx_ref[...])
    for c in range(n_ct):
        pad_ref[c, CONV_HALO:CONV_HALO + SSD_TS, :] = raw[:, c * LANES:(c + 1) * LANES]

    def conv(k):
        for c in range(n_ct):
            cols = slice(c * LANES, (c + 1) * LANES)
            acc = None
            for j in range(SSD_CONV):
                tap = jnp.broadcast_to(cw_ref[j:j + 1, cols], (SSD_CHUNK, LANES))
                base = CONV_HALO + k * SSD_CHUNK - (SSD_CONV - 1) + j
                term = tap * interleaved(lambda rows, c=c: pad_ref[c, rows, :], base)
                acc = term if acc is None else acc + term
            xc_ref[pl.ds(k * SSD_CHUNK, SSD_CHUNK), cols] = _silu(acc + cb_ref[:, cols])

    for k in range(n_chunks):
        conv(k)
    for c in range(n_ct):
        pad_ref[c, 0:CONV_HALO, :] = pad_ref[c, SSD_TS:SSD_TS + CONV_HALO, :]

    li = _interleaved_time(lax.broadcasted_iota(jnp.int32, (SSD_CHUNK, SSD_CHUNK), 0))
    si = _interleaved_time(lax.broadcasted_iota(jnp.int32, (SSD_CHUNK, SSD_CHUNK), 1))
    tril = li >= si
    ones_tril = jnp.where(tril, 1.0, 0.0).astype(BF16)
    first_head = lax.broadcasted_iota(jnp.int32, (SSD_CHUNK, LANES), 1) < SSD_HEADDIM
    gw = SSD_HPG * SSD_HEADDIM

    dt_ref[...] = _softplus(_dot(xb, wdt_ref[...]) + dtb_ref[...])
    a_neg = -jnp.exp(alog_ref[...])
    dts = []
    for k in range(n_chunks):
        dt_k = interleaved(lambda rows: dt_ref[rows, :], k * SSD_CHUNK)
        parts = _dot(ones_tril, jnp.concatenate(_split3_bf16(dt_k * a_neg), axis=1))
        acum_ref[pl.ds(k * SSD_CHUNK, SSD_CHUNK), :] = (
            (parts[:, 0:DT_PAD] + parts[:, DT_PAD:2 * DT_PAD]) + parts[:, 2 * DT_PAD:3 * DT_PAD])
        dts.append(dt_k)
    both = jnp.concatenate(dts + [acum_ref[...]], axis=0)
    ex2_ref[...] = _dot(jnp.concatenate(_split3_bf16(both), axis=1), ex_ref[...])

    def chunk(k):
        rows = pl.ds(k * SSD_CHUNK, SSD_CHUNK)
        acum = acum_ref[rows, :]
        acum_t = acum.T
        acx = ex2_ref[pl.ds(SSD_TS + k * SSD_CHUNK, SSD_CHUNK), :]
        a_last = acx[SSD_CHUNK - 1:SSD_CHUNK, :]
        xs = xc_ref[rows, 0:SSD_INNER]
        xdt = xs * ex2_ref[rows, :]
        xw_b = (xdt * jnp.exp(a_last - acx)).astype(BF16)
        e_cum = jnp.exp(acx)
        e_last = jnp.exp(a_last)
        for g in range(SSD_GROUPS):
            gc = slice(g * gw, (g + 1) * gw)
            b_lo = SSD_INNER + g * SSD_STATE
            c_lo = SSD_INNER + SSD_GROUPS * SSD_STATE + g * SSD_STATE
            bmb = xc_ref[rows, b_lo:b_lo + SSD_STATE].astype(BF16)
            cmb = xc_ref[rows, c_lo:c_lo + SSD_STATE].astype(BF16)
            cbm = lax.dot_general(cmb, bmb, _NT, preferred_element_type=F32)
            prev = st_ref[g]
            y_off = _dot(cmb, prev.astype(BF16)) * e_cum[:, gc]
            new = lax.dot_general(bmb, xw_b[:, gc], _TN, preferred_element_type=F32)
            st_ref[g] = prev * e_last[:, gc] + new
            for t in range(gw // LANES):
                tc = slice(g * gw + t * LANES, g * gw + (t + 1) * LANES)
                x_pair = xdt[:, tc]
                mix = []
                for e in range(LANES // SSD_HEADDIM):
                    h = g * SSD_HPG + t * (LANES // SSD_HEADDIM) + e
                    diff = acum[:, h:h + 1] - acum_t[h:h + 1, :]
                    decay = jnp.exp(jnp.where(tril, diff, -jnp.inf))
                    mix.append((cbm * decay).astype(BF16))
                x_split = jnp.concatenate([jnp.where(first_head, x_pair, 0.0).astype(BF16),
                                           jnp.where(first_head, 0.0, x_pair).astype(BF16)], axis=0)
                y_t = (_dot(jnp.concatenate(mix, axis=1), x_split) + y_off[:, t * LANES:(t + 1) * LANES]
                       + dsk_ref[:, tc] * xs[:, tc])
                for n, (u, v) in enumerate(units):
                    y_ref[g * (gw // LANES) + t,
                          pl.ds(k * SSD_CHUNK + u * CONV_UNIT + v, 8, stride=CONV_STRIDE), :] = y_t[n * 8:(n + 1) * 8, :]

    for k in range(n_chunks):
        chunk(k)

    z = _dot(xb, wz_ref[...])
    for g in range(SSD_GROUPS):
        cols = slice(g * gw, (g + 1) * gw)
        yg = jnp.concatenate([y_ref[g * (gw // LANES) + t] for t in range(gw // LANES)], axis=1)
        yg = yg * _silu(z[:, cols])
        ms = jnp.mean(yg * yg, axis=-1, keepdims=True)
        o_ref[:, cols] = (yg * lax.rsqrt(ms + SSD_EPS) * nw_ref[:, cols]).astype(BF16)


def _head_expander():
    e = (np.arange(SSD_INNER)[None, :] // SSD_HEADDIM == np.arange(DT_PAD)[:, None]).astype(np.float32)
    return jnp.asarray(np.concatenate([e, e, e], axis=0), BF16)


def _ssd(xb, wz, wx, wdt, cw, cb, dtb, alog, dsk, nw, layer):
    steps = SEQ // SSD_TS
    row = pl.BlockSpec((SSD_TS, D_MODEL), lambda b, s: (b * steps + s, 0))
    return pl.pallas_call(
        _ssd_body,
        grid=(BATCH, steps),
        in_specs=[row,
                  _layer_spec((D_MODEL, SSD_INNER), layer),
                  _layer_spec((D_MODEL, SSD_CONV_CH), layer),
                  _layer_spec((D_MODEL, DT_PAD), layer),
                  pl.BlockSpec((3 * DT_PAD, SSD_INNER), lambda b, s: (0, 0), pipeline_mode=pl.Buffered(1)),
                  _layer_spec((SSD_CONV, SSD_CONV_CH), layer),
                  _layer_spec((1, SSD_CONV_CH), layer),
                  _layer_spec((1, DT_PAD), layer),
                  _layer_spec((1, DT_PAD), layer),
                  _layer_spec((1, SSD_INNER), layer),
                  _layer_spec((1, SSD_INNER), layer)],
        out_specs=pl.BlockSpec((SSD_TS, SSD_INNER), lambda b, s: (b * steps + s, 0)),
        out_shape=jax.ShapeDtypeStruct((N_TOK, SSD_INNER), BF16),
        scratch_shapes=[pltpu.VMEM((SSD_CONV_CH // LANES, CONV_HALO + SSD_TS, LANES), F32),
                        pltpu.VMEM((SSD_TS, SSD_CONV_CH), F32),
                        pltpu.VMEM((SSD_TS, DT_PAD), F32),
                        pltpu.VMEM((SSD_TS, DT_PAD), F32),
                        pltpu.VMEM((2 * SSD_TS, SSD_INNER), F32),
                        pltpu.VMEM((SSD_INNER // LANES, SSD_TS, LANES), F32),
                        pltpu.VMEM((SSD_GROUPS, SSD_STATE, SSD_HPG * SSD_HEADDIM), F32)],
        compiler_params=_params("arbitrary", "arbitrary"),
        name="ssd_mixer",
    )(xb, wz, wx, wdt, _head_expander(), cw, cb, dtb, alog, dsk, nw)


def _attn_body(xb_ref, w_ref, bias_ref, o_ref, tile_ref, qh_ref, kb_ref, vb_ref, na_ref, ma_ref, da_ref):
    n_lt = ATTN_OUT // LANES
    hpt = LANES // ATTN_HEAD_DIM
    lane = lax.broadcasted_iota(jnp.int32, (ATTN_SPAN, LANES), 1)
    first_head = lane < ATTN_HEAD_DIM
    for t in range(n_lt):
        kb_ref[t, 0:ATTN_SPAN, :] = jnp.zeros((ATTN_SPAN, LANES), BF16)
        vb_ref[t, 0:ATTN_SPAN, 0:LANES] = jnp.zeros((ATTN_SPAN, LANES), BF16)
        vb_ref[t, :, LANES:2 * LANES] = jnp.ones((ATTN_SPAN + SEQ, LANES), BF16)

    for g in ATTN_ORDER:
        dil = ATTN_CONFIGS[g][1]
        sub = SEQ // dil
        n_blk = sub // ATTN_SPAN
        per = TM // dil
        wcol = g * 3 * ATTN_OUT
        first_rows = lax.broadcasted_iota(jnp.int32, (per, LANES), 1) < ATTN_HEAD_DIM

        for j in range(SEQ // TM):
            qkv = _dot(xb_ref[j * TM:(j + 1) * TM, :], w_ref[:, wcol:wcol + 3 * ATTN_OUT])
            for which in range(3):
                for t in range(n_lt):
                    c0 = which * ATTN_OUT + t * LANES
                    piece = qkv[:, c0:c0 + LANES]
                    if which == 0:
                        piece = piece * (ATTN_HEAD_DIM ** -0.5)
                    slot = (j * 3 + which) * n_lt + t
                    if dil > 1:
                        tile_ref[slot] = piece
                    for r in range(dil):
                        if dil > 1:
                            piece = tile_ref[slot, pl.ds(r, per, stride=dil), :]
                        if which == 0:
                            dst = pl.ds(r * sub + j * per, per)
                            qh_ref[t * hpt, dst, :] = jnp.where(first_rows, piece, 0.0).astype(BF16)
                            qh_ref[t * hpt + 1, dst, :] = jnp.where(first_rows, 0.0, piece).astype(BF16)
                        elif which == 1:
                            kb_ref[t, pl.ds(ATTN_SPAN + r * sub + j * per, per), :] = piece.astype(BF16)
                        else:
                            vb_ref[t, pl.ds(ATTN_SPAN + r * sub + j * per, per), 0:LANES] = piece.astype(BF16)

        def block(u, carry, g=g, n_blk=n_blk, dil=dil):
            r0 = pl.multiple_of(u * ATTN_SPAN, ATTN_SPAN)
            rows = pl.ds(r0, ATTN_SPAN)
            krows = pl.ds(r0, 2 * ATTN_SPAN)
            no_prev = ATTN_HEADS if n_blk == 1 else jnp.where(u % n_blk == 0, ATTN_HEADS, 0)
            nat = rows if dil == 1 else pl.ds(u // n_blk + (u % n_blk) * (ATTN_SPAN * dil), ATTN_SPAN, stride=dil)
            for t in range(n_lt):
                k2 = kb_ref[t, krows, :]
                v2 = vb_ref[t, krows, :]
                parts = []
                for e in range(hpt):
                    h = g * ATTN_GROUP_HEADS + t * hpt + e
                    s = (lax.dot_general(qh_ref[t * hpt + e, rows, :], k2, _NT, preferred_element_type=F32)
                         + bias_ref[h + no_prev])
                    m = jnp.max(s, axis=-1, keepdims=True)
                    pv = _dot(jnp.exp(s - m).astype(BF16), v2)
                    parts.append((pv[:, 0:LANES], jnp.broadcast_to(m, (ATTN_SPAN, LANES)), pv[:, LANES:2 * LANES]))
                n_g, m_g, d_g = (jnp.where(first_head, a, b) for a, b in zip(parts[0], parts[1]))
                if g == ATTN_ORDER[0]:
                    na_ref[t, nat, :] = n_g
                    ma_ref[t, nat, :] = m_g
                    da_ref[t, nat, :] = d_g
                else:
                    m_a = ma_ref[t, nat, :]
                    m_n = jnp.maximum(m_a, m_g)
                    e_a = jnp.exp(m_a - m_n)
                    e_g = jnp.exp(m_g - m_n)
                    n_n = na_ref[t, nat, :] * e_a + n_g * e_g
                    d_n = da_ref[t, nat, :] * e_a + d_g * e_g
                    if g == ATTN_ORDER[-1]:
                        o_ref[rows, t * LANES:(t + 1) * LANES] = (n_n / d_n).astype(BF16)
                    else:
                        na_ref[t, nat, :] = n_n
                        da_ref[t, nat, :] = d_n
                        ma_ref[t, nat, :] = m_n
            return carry

        lax.fori_loop(0, SEQ // ATTN_SPAN, block, 0, unroll=ATTN_UNROLL)


def _attn(xb, w, bias, layer):
    n_lt = ATTN_OUT // LANES
    return pl.pallas_call(
        _attn_body,
        grid=(BATCH,),
        in_specs=[pl.BlockSpec((SEQ, D_MODEL), lambda b: (b, 0)),
                  _layer_spec((D_MODEL, 3 * ATTN_WIDTH), layer),
                  pl.BlockSpec((2 * ATTN_HEADS, ATTN_SPAN, 2 * ATTN_SPAN), lambda b: (0, 0, 0),
                               pipeline_mode=pl.Buffered(1))],
        out_specs=pl.BlockSpec((SEQ, ATTN_OUT), lambda b: (b, 0)),
        out_shape=jax.ShapeDtypeStruct((N_TOK, ATTN_OUT), BF16),
        scratch_shapes=[pltpu.VMEM((SEQ // TM * 3 * n_lt, TM, LANES), F32),
                        pltpu.VMEM((ATTN_GROUP_HEADS, SEQ, LANES), BF16),
                        pltpu.VMEM((n_lt, ATTN_SPAN + SEQ, LANES), BF16),
                        pltpu.VMEM((n_lt, ATTN_SPAN + SEQ, 2 * LANES), BF16)]
                       + [pltpu.VMEM((n_lt, SEQ, LANES), F32)] * 3,
        compiler_params=_params("parallel"),
        name="dilated_attn",
    )(xb, w, bias)


def _merge_body(x_ref, ya_ref, yb_ref, yc_ref, wg_ref, gb_ref, pa_ref, pb_ref, pc_ref, wo_ref, g_ref, b_ref, o_ref,
                m_ref):
    for half in range(FFN_TM // TM):
        rows = slice(half * TM, (half + 1) * TM)
        x = x_ref[rows, :]
        xb = x.astype(BF16)
        for c in range(D_MODEL // MXU_DIM):
            cs = slice(c * MXU_DIM, (c + 1) * MXU_DIM)
            acc = None
            for i, (y_ref, p_ref) in enumerate(((ya_ref, pa_ref), (yb_ref, pb_ref), (yc_ref, pc_ref))):
                gs = slice(i * D_MODEL + c * MXU_DIM, i * D_MODEL + (c + 1) * MXU_DIM)
                gate = _sigmoid(_dot(xb, wg_ref[:, gs]) + gb_ref[i:i + 1, cs])
                term = gate * _dot(y_ref[rows, :], p_ref[:, cs])
                acc = term if acc is None else acc + term
            m_ref[half, :, cs] = acc.astype(BF16)
        mix = _dot(m_ref[half], wo_ref[...])
        o_ref[rows, :] = _layer_norm(DN_ALPHA * x + mix, g_ref[...], b_ref[...])


def _merge(x, ya, yb, yc, wg, gb, pa, pb, pc, wo, g, b, layer):
    def row(width):
        return pl.BlockSpec((FFN_TM, width), lambda t: (t, 0))

    return pl.pallas_call(
        _merge_body,
        grid=(N_TOK // FFN_TM,),
        in_specs=[row(D_MODEL), row(POOL_PAD), row(SSD_INNER), row(ATTN_OUT),
                  _layer_spec((D_MODEL, N_BRANCH * D_MODEL), layer),
                  _layer_spec((N_BRANCH, D_MODEL), layer),
                  _layer_spec((POOL_PAD, D_MODEL), layer),
                  _layer_spec((SSD_INNER, D_MODEL), layer),
                  _layer_spec((ATTN_OUT, D_MODEL), layer),
                  _layer_spec((D_MODEL, D_MODEL), layer),
                  _layer_spec((1, D_MODEL), layer),
                  _layer_spec((1, D_MODEL), layer)],
        out_specs=row(D_MODEL),
        out_shape=jax.ShapeDtypeStruct((N_TOK, D_MODEL), F32),
        scratch_shapes=[pltpu.VMEM((FFN_TM // TM, TM, D_MODEL), BF16)],
        compiler_params=_params("parallel"),
        name="gated_merge",
    )(x, ya, yb, yc, wg, gb, pa, pb, pc, wo, g, b)


def _t5_bucket(dist):
    dist = np.maximum(dist, 0)
    max_exact = REL_BUCKETS // 2
    large = max_exact + (np.log(np.maximum(dist, 1) / max_exact) / np.log(REL_MAX_DIST / max_exact)
                         * (REL_BUCKETS - max_exact)).astype(np.int32)
    large = np.minimum(large, REL_BUCKETS - 1)
    return np.where(dist < max_exact, dist, large).astype(np.int32)


def _attn_bias(rel_bias):
    qi = np.arange(ATTN_SPAN)[:, None]
    kj = np.arange(2 * ATTN_SPAN)[None, :]
    delta = qi - kj + ATTN_SPAN
    in_band = (delta >= 0) & (delta <= ATTN_SPAN)
    mats = []
    for gi, (_, dil) in enumerate(ATTN_CONFIGS):
        tab = rel_bias[:, gi * ATTN_GROUP_HEADS:(gi + 1) * ATTN_GROUP_HEADS].astype(F32)
        bucket = jnp.asarray(np.where(in_band, _t5_bucket(delta * dil), -1))
        onehot = (bucket[..., None] == jnp.arange(REL_BUCKETS)).astype(F32)
        b = jnp.einsum('qkb,bh->hqk', onehot, tab, precision=lax.Precision.HIGHEST)
        mats.append(jnp.where(jnp.asarray(in_band), b, NEG_BIG))
    bias = jnp.concatenate(mats, axis=0)
    prev_cols = jnp.asarray(kj < ATTN_SPAN)
    return jnp.concatenate([bias, jnp.where(prev_cols, NEG_BIG, bias)], axis=0)


def _pad_groups(a, axis):
    shp = a.shape
    a = a.reshape(shp[:axis] + (POOL_GROUPS, POOL_GDIM) + shp[axis + 1:])
    pad = [(0, 0)] * a.ndim
    pad[axis + 1] = (0, POOL_GPAD - POOL_GDIM)
    a = jnp.pad(a, pad)
    return a.reshape(shp[:axis] + (POOL_PAD,) + shp[axis + 1:])


def kernel(x, ffn1_w13, ffn1_w2, ln1_g, ln1_b, w_in, gate_b, pool_w, pool_b, pool_scale, conv_w, conv_b, dt_bias, a_log, d_skip, ssd_norm, rel_bias, p_pool, p_ssd, p_attn, w_out, ln2_g, ln2_b, ffn2_w13, ffn2_w2, ln3_g, ln3_b):
    sec = [w_in[:, :, IN_OFFS[i]:IN_OFFS[i + 1]] for i in range(len(IN_SIZES))]
    w_u, w_z, w_xbc, w_dt, w_q, w_k, w_v, w_g = sec
    w_pool = _pad_groups(w_u, 2).astype(BF16)
    w_z = w_z.astype(BF16)
    w_xbc = w_xbc.astype(BF16)
    w_dt = jnp.pad(w_dt, ((0, 0), (0, 0), (0, DT_PAD - SSD_HEADS))).astype(BF16)
    w_qkv = jnp.concatenate(
        [s[:, :, gi * ATTN_OUT:(gi + 1) * ATTN_OUT] for gi in range(len(ATTN_CONFIGS)) for s in (w_q, w_k, w_v)],
        axis=2).astype(BF16)
    w_g = w_g.astype(BF16)

    pw = jnp.pad(pool_w, ((0, 0), (0, 0), (0, POOL_GPAD - POOL_GDIM), (0, POOL_GPAD - POOL_GDIM))).astype(BF16)
    pb = jnp.pad(pool_b, ((0, 0), (0, 0), (0, POOL_GPAD - POOL_GDIM)))
    ps = jnp.pad(pool_scale.reshape(DEPTH, POOL_GROUPS, POOL_GDIM), ((0, 0), (0, 0), (0, POOL_GPAD - POOL_GDIM)))
    p_pool_b = _pad_groups(p_pool, 1).astype(BF16)
    p_ssd_b = p_ssd.astype(BF16)
    p_attn_b = p_attn.astype(BF16)
    w_out_b = w_out.astype(BF16)

    dtb = jnp.pad(dt_bias, ((0, 0), (0, DT_PAD - SSD_HEADS)))[:, None, :]
    alog = jnp.pad(a_log, ((0, 0), (0, DT_PAD - SSD_HEADS)))[:, None, :]
    dsk = jnp.repeat(d_skip, SSD_HEADDIM, axis=1)[:, None, :]
    cb = conv_b[:, None, :]
    nw = ssd_norm[:, None, :]
    bias = _attn_bias(rel_bias)

    f1_w13, f1_w2 = ffn1_w13.astype(BF16), ffn1_w2.astype(BF16)
    f2_w13, f2_w2 = ffn2_w13.astype(BF16), ffn2_w2.astype(BF16)
    vec = lambda a: a[:, None, :]

    h = x.reshape(N_TOK, D_MODEL)
    for i in range(DEPTH):
        h, hb = _ffn(h, f1_w13, f1_w2, vec(ln1_g), vec(ln1_b), i, True)
        ya = _pool(hb, w_pool, pw, pb, ps, i)
        yb = _ssd(hb, w_z, w_xbc, w_dt, conv_w, cb, dtb, alog, dsk, nw, i)
        yc = _attn(hb, w_qkv, bias, i)
        h = _merge(h, ya, yb, yc, w_g, gate_b, p_pool_b, p_ssd_b, p_attn_b, w_out_b, vec(ln2_g), vec(ln2_b), i)
        (h,) = _ffn(h, f2_w13, f2_w2, vec(ln3_g), vec(ln3_b), i, False)
    return h.reshape(BATCH, SEQ, D_MODEL)
```

```python
import functools

import numpy as np
import jax
import jax.numpy as jnp
from jax import lax
from jax.experimental import pallas as pl
from jax.experimental.pallas import tpu as pltpu

F32 = jnp.float32
BF16 = jnp.bfloat16

D_MODEL = 1024
BATCH = 16
SEQ = 2048
DEPTH = 4
N_TOK = BATCH * SEQ
LN_EPS = 1e-5
DN_ALPHA = (2.0 * DEPTH) ** 0.25
FFN_RES = 0.5
D_FF = ((8 * D_MODEL // 3 + 127) // 128) * 128
POOL_WIDTH = 3 * D_MODEL // 4
POOL_WINDOWS = (2, 4, 8, 16)
POOL_GROUPS = len(POOL_WINDOWS)
POOL_GDIM = POOL_WIDTH // POOL_GROUPS
SSD_INNER = D_MODEL
SSD_HEADDIM = 64
SSD_HEADS = SSD_INNER // SSD_HEADDIM
SSD_GROUPS = 4
SSD_HPG = SSD_HEADS // SSD_GROUPS
SSD_STATE = 128
SSD_CONV = 4
SSD_CHUNK = 128
SSD_CONV_CH = SSD_INNER + 2 * SSD_GROUPS * SSD_STATE
SSD_EPS = 1e-5
ATTN_CONFIGS = ((128, 1), (512, 4), (2048, 16))
ATTN_HEAD_DIM = 64
ATTN_GROUP_HEADS = 4
ATTN_HEADS = ATTN_GROUP_HEADS * len(ATTN_CONFIGS)
ATTN_WIDTH = ATTN_HEADS * ATTN_HEAD_DIM
ATTN_OUT = ATTN_GROUP_HEADS * ATTN_HEAD_DIM
ATTN_SPAN = 128
REL_BUCKETS = 32
REL_MAX_DIST = 2048
N_BRANCH = 3
IN_SIZES = (POOL_WIDTH, SSD_INNER, SSD_CONV_CH, SSD_HEADS, ATTN_WIDTH, ATTN_WIDTH, ATTN_WIDTH, N_BRANCH * D_MODEL)
IN_OFFS = tuple(sum(IN_SIZES[:i]) for i in range(len(IN_SIZES) + 1))

LANES = 128
MXU_DIM = 256
VMEM_LIMIT = 56 * 1024 * 1024

TM = 512
FFN_TM = 1024
FFN_CHUNK = MXU_DIM
POOL_GPAD = MXU_DIM
POOL_PAD = POOL_GROUPS * POOL_GPAD
POOL_HALO = 16
POOL_ROWS = 512
DT_PAD = LANES
CONV_HALO = 8
SSD_TS = 512
CONV_STRIDE = 4
CONV_UNIT = 8 * CONV_STRIDE
NEG_BIG = -1e30
ATTN_UNROLL = 16
ATTN_ORDER = (2, 1, 0)
assert ATTN_CONFIGS[ATTN_ORDER[-1]][1] == 1

assert all(w // d == ATTN_SPAN for w, d in ATTN_CONFIGS)
assert POOL_HALO >= max(POOL_WINDOWS) and CONV_HALO >= SSD_CONV - 1

_NT = (((1,), (1,)), ((), ()))
_TN = (((0,), (0,)), ((), ()))


def _dot(a, b):
    return jnp.dot(a, b, preferred_element_type=F32)


def _silu(a):
    h = 0.5 * a
    return h + h * jnp.tanh(h)


def _sigmoid(a):
    return 0.5 + 0.5 * jnp.tanh(0.5 * a)


def _layer_norm(r, g, b):
    mu = jnp.mean(r, axis=-1, keepdims=True)
    c = r - mu
    var = jnp.mean(c * c, axis=-1, keepdims=True)
    return c * lax.rsqrt(var + LN_EPS) * g + b


def _layer_spec(block, layer):
    nd = len(block)
    return pl.BlockSpec((None,) + tuple(block), lambda *_: (layer,) + (0,) * nd,
                        pipeline_mode=pl.Buffered(1))


def _params(*sem):
    return pltpu.CompilerParams(dimension_semantics=sem, vmem_limit_bytes=VMEM_LIMIT)


def _ffn_body(x_ref, w13_ref, w2_ref, g_ref, b_ref, *rest, emit_bf16):
    if emit_bf16:
        o_ref, ob_ref, h_ref = rest
    else:
        o_ref, h_ref = rest
    for half in range(FFN_TM // TM):
        rows = slice(half * TM, (half + 1) * TM)
        x = x_ref[rows, :]
        xb = x.astype(BF16)
        for c in range(D_FF // FFN_CHUNK):
            lo = c * FFN_CHUNK
            a = _dot(xb, w13_ref[:, lo:lo + FFN_CHUNK])
            g = _dot(xb, w13_ref[:, D_FF + lo:D_FF + lo + FFN_CHUNK])
            h_ref[half, :, lo:lo + FFN_CHUNK] = (_silu(a) * g).astype(BF16)
        y = _dot(h_ref[half], w2_ref[...])
        out = _layer_norm(DN_ALPHA * x + FFN_RES * y, g_ref[...], b_ref[...])
        o_ref[rows, :] = out
        if emit_bf16:
            ob_ref[rows, :] = out.astype(BF16)


def _ffn(x, w13, w2, g, b, layer, emit_bf16):
    row = pl.BlockSpec((FFN_TM, D_MODEL), lambda t: (t, 0))
    out_shape = [jax.ShapeDtypeStruct((N_TOK, D_MODEL), F32)]
    out_specs = [row]
    if emit_bf16:
        out_shape.append(jax.ShapeDtypeStruct((N_TOK, D_MODEL), BF16))
        out_specs.append(row)
    return pl.pallas_call(
        functools.partial(_ffn_body, emit_bf16=emit_bf16),
        grid=(N_TOK // FFN_TM,),
        in_specs=[row,
                  _layer_spec((D_MODEL, 2 * D_FF), layer),
                  _layer_spec((D_FF, D_MODEL), layer),
                  _layer_spec((1, D_MODEL), layer),
                  _layer_spec((1, D_MODEL), layer)],
        out_specs=out_specs,
        out_shape=out_shape,
        scratch_shapes=[pltpu.VMEM((FFN_TM // TM, TM, D_FF), BF16)],
        compiler_params=_params("parallel"),
        name="ffn",
    )(x, w13, w2, g, b)


def _pool_body(xb_ref, w_ref, pw_ref, pb_ref, ps_ref, o_ref, u_ref):
    u_ref[0:POOL_HALO, :] = jnp.zeros((POOL_HALO, POOL_PAD), F32)

    def proj(j):
        u_ref[POOL_HALO + j * TM:POOL_HALO + (j + 1) * TM, :] = _dot(xb_ref[j * TM:(j + 1) * TM, :], w_ref[...])

    def chunk(c):
        r0 = c * POOL_ROWS
        t = r0 + lax.broadcasted_iota(jnp.int32, (POOL_ROWS, 1), 0)
        for g, win in enumerate(POOL_WINDOWS):
            cols = slice(g * POOL_GPAD, (g + 1) * POOL_GPAD)
            ug = u_ref[r0:r0 + POOL_ROWS + POOL_HALO, cols]
            s = ug + pltpu.roll(ug, 1, 0)
            k = 2
            while k < win:
                s = s + pltpu.roll(s, k, 0)
                k *= 2
            inv = 1.0 / jnp.minimum(t + 1, win).astype(F32)
            pooled = s[POOL_HALO:] * inv - ug[POOL_HALO:]
            y = _dot(pooled.astype(BF16), pw_ref[g])
            y = (y + pb_ref[g:g + 1, :]) * ps_ref[g:g + 1, :]
            o_ref[r0:r0 + POOL_ROWS, cols] = y.astype(BF16)

    per_tile = TM // POOL_ROWS
    for j in range(SEQ // TM + 1):
        if j < SEQ // TM:
            proj(j)
        if j > 0:
            for c in range((j - 1) * per_tile, j * per_tile):
                chunk(c)


def _pool(xb, w, pw, pb, ps, layer):
    seq = pl.BlockSpec((SEQ, D_MODEL), lambda b: (b, 0))
    return pl.pallas_call(
        _pool_body,
        grid=(BATCH,),
        in_specs=[seq,
                  _layer_spec((D_MODEL, POOL_PAD), layer),
                  _layer_spec((POOL_GROUPS, POOL_GPAD, POOL_GPAD), layer),
                  _layer_spec((POOL_GROUPS, POOL_GPAD), layer),
                  _layer_spec((POOL_GROUPS, POOL_GPAD), layer)],
        out_specs=pl.BlockSpec((SEQ, POOL_PAD), lambda b: (b, 0)),
        out_shape=jax.ShapeDtypeStruct((N_TOK, POOL_PAD), BF16),
        scratch_shapes=[pltpu.VMEM((POOL_HALO + SEQ, POOL_PAD), F32)],
        compiler_params=_params("parallel"),
        name="pool_mixer",
    )(xb, w, pw, pb, ps)


def _split3_bf16(v):
    hi = v.astype(BF16)
    r1 = v - hi.astype(F32)
    mid = r1.astype(BF16)
    lo = (r1 - mid.astype(F32)).astype(BF16)
    return hi, mid, lo


def _softplus(v):
    return jnp.maximum(v, 0.0) + jnp.log1p(jnp.exp(-jnp.abs(v)))


def _interleaved_time(i):
    return (i & ~(CONV_UNIT - 1)) | ((i & 7) << 2) | ((i >> 3) & (CONV_STRIDE - 1))


def _ssd_body(xb_ref, wz_ref, wx_ref, wdt_ref, ex_ref, cw_ref, cb_ref, dtb_ref, alog_ref, dsk_ref, nw_ref,
              o_ref, pad_ref, xc_ref, dt_ref, acum_ref, ex2_ref, y_ref, st_ref):
    n_ct = SSD_CONV_CH // LANES
    n_chunks = SSD_TS // SSD_CHUNK
    units = [(u, v) for u in range(SSD_CHUNK // CONV_UNIT) for v in range(CONV_STRIDE)]

    @pl.when(pl.program_id(1) == 0)
    def _():
        st_ref[...] = jnp.zeros(st_ref.shape, F32)
        for c in range(n_ct):
            pad_ref[c, 0:CONV_HALO, :] = jnp.zeros((CONV_HALO, LANES), F32)

    def interleaved(ref_rows, base):
        return jnp.concatenate([ref_rows(pl.ds(base + u * CONV_UNIT + v, 8, stride=CONV_STRIDE))
                                for u, v in units], axis=0)

    xb = xb_ref[...]
    raw = _dot(xb, wx_ref[...])
    for c in range(n_ct):
        pad_ref[c, CONV_HALO:CONV_HALO + SSD_TS, :] = raw[:, c * LANES:(c + 1) * LANES]

    def conv(k):
        for c in range(n_ct):
            cols = slice(c * LANES, (c + 1) * LANES)
            acc = None
            for j in range(SSD_CONV):
                tap = jnp.broadcast_to(cw_ref[j:j + 1, cols], (SSD_CHUNK, LANES))
                base = CONV_HALO + k * SSD_CHUNK - (SSD_CONV - 1) + j
                term = tap * interleaved(lambda rows, c=c: pad_ref[c, rows, :], base)
                acc = term if acc is None else acc + term
            xc_ref[pl.ds(k * SSD_CHUNK, SSD_CHUNK), cols] = _silu(acc + cb_ref[:, cols])

    for k in range(n_chunks):
        conv(k)
    for c in range(n_ct):
        pad_ref[c, 0:CONV_HALO, :] = pad_ref[c, SSD_TS:SSD_TS + CONV_HALO, :]

    li = _interleaved_time(lax.broadcasted_iota(jnp.int32, (SSD_CHUNK, SSD_CHUNK), 0))
    si = _interleaved_time(lax.broadcasted_iota(jnp.int32, (SSD_CHUNK, SSD_CHUNK), 1))
    tril = li >= si
    ones_tril = jnp.where(tril, 1.0, 0.0).astype(BF16)
    first_head = lax.broadcasted_iota(jnp.int32, (SSD_CHUNK, LANES), 1) < SSD_HEADDIM
    gw = SSD_HPG * SSD_HEADDIM

    dt_ref[...] = _softplus(_dot(xb, wdt_ref[...]) + dtb_ref[...])
    a_neg = -jnp.exp(alog_ref[...])
    dts, tails = [], []
    for k in range(n_chunks):
        dt_k = interleaved(lambda rows: dt_ref[rows, :], k * SSD_CHUNK)
        parts = _dot(ones_tril, jnp.concatenate(_split3_bf16(dt_k * a_neg), axis=1))
        acum_k = (parts[:, 0:DT_PAD] + parts[:, DT_PAD:2 * DT_PAD]) + parts[:, 2 * DT_PAD:3 * DT_PAD]
        acum_ref[pl.ds(k * SSD_CHUNK, SSD_CHUNK), :] = acum_k
        dts.append(dt_k)
        tails.append(acum_k[SSD_CHUNK - 1:SSD_CHUNK, :] - acum_k)
    trio = jnp.concatenate(dts + [acum_ref[...]] + tails, axis=0)
    hi = trio.astype(BF16)
    mid = (trio - hi.astype(F32)).astype(BF16)
    ex2_ref[...] = _dot(jnp.concatenate([hi, mid], axis=1), ex_ref[...])

    def chunk(k):
        rows = pl.ds(k * SSD_CHUNK, SSD_CHUNK)
        acum = acum_ref[rows, :]
        acum_t = acum.T
        acx = ex2_ref[pl.ds(SSD_TS + k * SSD_CHUNK, SSD_CHUNK), :]
        xs = xc_ref[rows, 0:SSD_INNER]
        xdt = xs * ex2_ref[rows, :]
        xw_b = (xdt * jnp.exp(ex2_ref[pl.ds(2 * SSD_TS + k * SSD_CHUNK, SSD_CHUNK), :])).astype(BF16)
        e_cum = jnp.exp(acx)
        e_last = e_cum[SSD_CHUNK - 1:SSD_CHUNK, :]
        for g in range(SSD_GROUPS):
            gc = slice(g * gw, (g + 1) * gw)
            b_lo = SSD_INNER + g * SSD_STATE
            c_lo = SSD_INNER + SSD_GROUPS * SSD_STATE + g * SSD_STATE
            bmb = xc_ref[rows, b_lo:b_lo + SSD_STATE].astype(BF16)
            cmb = xc_ref[rows, c_lo:c_lo + SSD_STATE].astype(BF16)
            cbm = lax.dot_general(cmb, bmb, _NT, preferred_element_type=F32)
            prev = st_ref[g]
            y_off = _dot(cmb, prev.astype(BF16)) * e_cum[:, gc]
            new = lax.dot_general(bmb, xw_b[:, gc], _TN, preferred_element_type=F32)
            st_ref[g] = prev * e_last[:, gc] + new
            for t in range(gw // LANES):
                tc = slice(g * gw + t * LANES, g * gw + (t + 1) * LANES)
                x_pair = xdt[:, tc]
                mix = []
                for e in range(LANES // SSD_HEADDIM):
                    h = g * SSD_HPG + t * (LANES // SSD_HEADDIM) + e
                    diff = acum[:, h:h + 1] - acum_t[h:h + 1, :]
                    decay = jnp.exp(jnp.where(tril, diff, -jnp.inf))
                    mix.append((cbm * decay).astype(BF16))
                x_split = jnp.concatenate([jnp.where(first_head, x_pair, 0.0).astype(BF16),
                                           jnp.where(first_head, 0.0, x_pair).astype(BF16)], axis=0)
                y_t = (_dot(jnp.concatenate(mix, axis=1), x_split) + y_off[:, t * LANES:(t + 1) * LANES]
                       + dsk_ref[:, tc] * xs[:, tc])
                for n, (u, v) in enumerate(units):
                    y_ref[g * (gw // LANES) + t,
                          pl.ds(k * SSD_CHUNK + u * CONV_UNIT + v, 8, stride=CONV_STRIDE), :] = y_t[n * 8:(n + 1) * 8, :]

    for k in range(n_chunks):
        chunk(k)

    z = _dot(xb, wz_ref[...])
    for g in range(SSD_GROUPS):
        cols = slice(g * gw, (g + 1) * gw)
        yg = jnp.concatenate([y_ref[g * (gw // LANES) + t] for t in range(gw // LANES)], axis=1)
        yg = yg * _silu(z[:, cols])
        ms = jnp.mean(yg * yg, axis=-1, keepdims=True)
        o_ref[:, cols] = (yg * lax.rsqrt(ms + SSD_EPS) * nw_ref[:, cols]).astype(BF16)


def _head_expander():
    e = (np.arange(SSD_INNER)[None, :] // SSD_HEADDIM == np.arange(DT_PAD)[:, None]).astype(np.float32)
    return jnp.asarray(np.concatenate([e, e], axis=0), BF16)


def _ssd(xb, wz, wx, wdt, cw, cb, dtb, alog, dsk, nw, layer):
    steps = SEQ // SSD_TS
    row = pl.BlockSpec((SSD_TS, D_MODEL), lambda b, s: (b * steps + s, 0))
    return pl.pallas_call(
        _ssd_body,
        grid=(BATCH, steps),
        in_specs=[row,
                  _layer_spec((D_MODEL, SSD_INNER), layer),
                  _layer_spec((D_MODEL, SSD_CONV_CH), layer),
                  _layer_spec((D_MODEL, DT_PAD), layer),
                  pl.BlockSpec((2 * DT_PAD, SSD_INNER), lambda b, s: (0, 0), pipeline_mode=pl.Buffered(1)),
                  _layer_spec((SSD_CONV, SSD_CONV_CH), layer),
                  _layer_spec((1, SSD_CONV_CH), layer),
                  _layer_spec((1, DT_PAD), layer),
                  _layer_spec((1, DT_PAD), layer),
                  _layer_spec((1, SSD_INNER), layer),
                  _layer_spec((1, SSD_INNER), layer)],
        out_specs=pl.BlockSpec((SSD_TS, SSD_INNER), lambda b, s: (b * steps + s, 0)),
        out_shape=jax.ShapeDtypeStruct((N_TOK, SSD_INNER), BF16),
        scratch_shapes=[pltpu.VMEM((SSD_CONV_CH // LANES, CONV_HALO + SSD_TS, LANES), F32),
                        pltpu.VMEM((SSD_TS, SSD_CONV_CH), F32),
                        pltpu.VMEM((SSD_TS, DT_PAD), F32),
                        pltpu.VMEM((SSD_TS, DT_PAD), F32),
                        pltpu.VMEM((3 * SSD_TS, SSD_INNER), F32),
                        pltpu.VMEM((SSD_INNER // LANES, SSD_TS, LANES), F32),
                        pltpu.VMEM((SSD_GROUPS, SSD_STATE, SSD_HPG * SSD_HEADDIM), F32)],
        compiler_params=_params("arbitrary", "arbitrary"),
        name="ssd_mixer",
    )(xb, wz, wx, wdt, _head_expander(), cw, cb, dtb, alog, dsk, nw)


def _attn_body(xb_ref, w_ref, bias_ref, o_ref, tile_ref, qh_ref, kb_ref, vb_ref, na_ref, ma_ref, da_ref):
    n_lt = ATTN_OUT // LANES
    hpt = LANES // ATTN_HEAD_DIM
    lane = lax.broadcasted_iota(jnp.int32, (ATTN_SPAN, LANES), 1)
    first_head = lane < ATTN_HEAD_DIM
    for t in range(n_lt):
        kb_ref[t, 0:ATTN_SPAN, :] = jnp.zeros((ATTN_SPAN, LANES), BF16)
        vb_ref[t, 0:ATTN_SPAN, 0:LANES] = jnp.zeros((ATTN_SPAN, LANES), BF16)
        vb_ref[t, :, LANES:2 * LANES] = jnp.ones((ATTN_SPAN + SEQ, LANES), BF16)

    for g in ATTN_ORDER:
        dil = ATTN_CONFIGS[g][1]
        sub = SEQ // dil
        n_blk = sub // ATTN_SPAN
        per = TM // dil
        wcol = g * 3 * ATTN_OUT
        first_rows = lax.broadcasted_iota(jnp.int32, (per, LANES), 1) < ATTN_HEAD_DIM

        for j in range(SEQ // TM):
            qkv = _dot(xb_ref[j * TM:(j + 1) * TM, :], w_ref[:, wcol:wcol + 3 * ATTN_OUT])
            for which in range(3):
                for t in range(n_lt):
                    c0 = which * ATTN_OUT + t * LANES
                    piece = qkv[:, c0:c0 + LANES]
                    if which == 0:
                        piece = piece * (ATTN_HEAD_DIM ** -0.5)
                    slot = (j * 3 + which) * n_lt + t
                    if dil > 1:
                        tile_ref[slot] = piece
                    for r in range(dil):
                        if dil > 1:
                            piece = tile_ref[slot, pl.ds(r, per, stride=dil), :]
                        if which == 0:
                            dst = pl.ds(r * sub + j * per, per)
                            qh_ref[t * hpt, dst, :] = jnp.where(first_rows, piece, 0.0).astype(BF16)
                            qh_ref[t * hpt + 1, dst, :] = jnp.where(first_rows, 0.0, piece).astype(BF16)
                        elif which == 1:
                            kb_ref[t, pl.ds(ATTN_SPAN + r * sub + j * per, per), :] = piece.astype(BF16)
                        else:
                            vb_ref[t, pl.ds(ATTN_SPAN + r * sub + j * per, per), 0:LANES] = piece.astype(BF16)

        def block(u, carry, g=g, n_blk=n_blk, dil=dil):
            r0 = pl.multiple_of(u * ATTN_SPAN, ATTN_SPAN)
            rows = pl.ds(r0, ATTN_SPAN)
            krows = pl.ds(r0, 2 * ATTN_SPAN)
            no_prev = ATTN_HEADS if n_blk == 1 else jnp.where(u % n_blk == 0, ATTN_HEADS, 0)
            nat = rows if dil == 1 else pl.ds(u // n_blk + (u % n_blk) * (ATTN_SPAN * dil), ATTN_SPAN, stride=dil)
            for t in range(n_lt):
                k2 = kb_ref[t, krows, :]
                v2 = vb_ref[t, krows, :]
                parts = []
                for e in range(hpt):
                    h = g * ATTN_GROUP_HEADS + t * hpt + e
                    s = (lax.dot_general(qh_ref[t * hpt + e, rows, :], k2, _NT, preferred_element_type=F32)
                         + bias_ref[h + no_prev])
                    m = jnp.max(s, axis=-1, keepdims=True)
                    pv = _dot(jnp.exp(s - m).astype(BF16), v2)
                    parts.append((pv[:, 0:LANES], jnp.broadcast_to(m, (ATTN_SPAN, LANES)), pv[:, LANES:2 * LANES]))
                n_g, m_g, d_g = (jnp.where(first_head, a, b) for a, b in zip(parts[0], parts[1]))
                if g == ATTN_ORDER[0]:
                    na_ref[t, nat, :] = n_g
                    ma_ref[t, nat, :] = m_g
                    da_ref[t, nat, :] = d_g
                else:
                    m_a = ma_ref[t, nat, :]
                    m_n = jnp.maximum(m_a, m_g)
                    e_a = jnp.exp(m_a - m_n)
                    e_g = jnp.exp(m_g - m_n)
                    n_n = na_ref[t, nat, :] * e_a + n_g * e_g
                    d_n = da_ref[t, nat, :] * e_a + d_g * e_g
                    if g == ATTN_ORDER[-1]:
                        o_ref[rows, t * LANES:(t + 1) * LANES] = (n_n / d_n).astype(BF16)
                    else:
                        na_ref[t, nat, :] = n_n
                        da_ref[t, nat, :] = d_n
                        ma_ref[t, nat, :] = m_n
            return carry

        lax.fori_loop(0, SEQ // ATTN_SPAN, block, 0, unroll=ATTN_UNROLL)


def _attn(xb, w, bias, layer):
    n_lt = ATTN_OUT // LANES
    return pl.pallas_call(
        _attn_body,
        grid=(BATCH,),
        in_specs=[pl.BlockSpec((SEQ, D_MODEL), lambda b: (b, 0)),
                  _layer_spec((D_MODEL, 3 * ATTN_WIDTH), layer),
                  pl.BlockSpec((2 * ATTN_HEADS, ATTN_SPAN, 2 * ATTN_SPAN), lambda b: (0, 0, 0),
                               pipeline_mode=pl.Buffered(1))],
        out_specs=pl.BlockSpec((SEQ, ATTN_OUT), lambda b: (b, 0)),
        out_shape=jax.ShapeDtypeStruct((N_TOK, ATTN_OUT), BF16),
        scratch_shapes=[pltpu.VMEM((SEQ // TM * 3 * n_lt, TM, LANES), F32),
                        pltpu.VMEM((ATTN_GROUP_HEADS, SEQ, LANES), BF16),
                        pltpu.VMEM((n_lt, ATTN_SPAN + SEQ, LANES), BF16),
                        pltpu.VMEM((n_lt, ATTN_SPAN + SEQ, 2 * LANES), BF16)]
                       + [pltpu.VMEM((n_lt, SEQ, LANES), F32)] * 3,
        compiler_params=_params("parallel"),
        name="dilated_attn",
    )(xb, w, bias)


def _merge_body(x_ref, ya_ref, yb_ref, yc_ref, wg_ref, gb_ref, pa_ref, pb_ref, pc_ref, wo_ref, g_ref, b_ref, o_ref,
                m_ref):
    for half in range(FFN_TM // TM):
        rows = slice(half * TM, (half + 1) * TM)
        x = x_ref[rows, :]
        xb = x.astype(BF16)
        for c in range(D_MODEL // MXU_DIM):
            cs = slice(c * MXU_DIM, (c + 1) * MXU_DIM)
            acc = None
            for i, (y_ref, p_ref) in enumerate(((ya_ref, pa_ref), (yb_ref, pb_ref), (yc_ref, pc_ref))):
                gs = slice(i * D_MODEL + c * MXU_DIM, i * D_MODEL + (c + 1) * MXU_DIM)
                gate = _sigmoid(_dot(xb, wg_ref[:, gs]) + gb_ref[i:i + 1, cs])
                term = gate * _dot(y_ref[rows, :], p_ref[:, cs])
                acc = term if acc is None else acc + term
            m_ref[half, :, cs] = acc.astype(BF16)
        mix = _dot(m_ref[half], wo_ref[...])
        o_ref[rows, :] = _layer_norm(DN_ALPHA * x + mix, g_ref[...], b_ref[...])


def _merge(x, ya, yb, yc, wg, gb, pa, pb, pc, wo, g, b, layer):
    def row(width):
        return pl.BlockSpec((FFN_TM, width), lambda t: (t, 0))

    return pl.pallas_call(
        _merge_body,
        grid=(N_TOK // FFN_TM,),
        in_specs=[row(D_MODEL), row(POOL_PAD), row(SSD_INNER), row(ATTN_OUT),
                  _layer_spec((D_MODEL, N_BRANCH * D_MODEL), layer),
                  _layer_spec((N_BRANCH, D_MODEL), layer),
                  _layer_spec((POOL_PAD, D_MODEL), layer),
                  _layer_spec((SSD_INNER, D_MODEL), layer),
                  _layer_spec((ATTN_OUT, D_MODEL), layer),
                  _layer_spec((D_MODEL, D_MODEL), layer),
                  _layer_spec((1, D_MODEL), layer),
                  _layer_spec((1, D_MODEL), layer)],
        out_specs=row(D_MODEL),
        out_shape=jax.ShapeDtypeStruct((N_TOK, D_MODEL), F32),
        scratch_shapes=[pltpu.VMEM((FFN_TM // TM, TM, D_MODEL), BF16)],
        compiler_params=_params("parallel"),
        name="gated_merge",
    )(x, ya, yb, yc, wg, gb, pa, pb, pc, wo, g, b)


def _t5_bucket(dist):
    dist = np.maximum(dist, 0)
    max_exact = REL_BUCKETS // 2
    large = max_exact + (np.log(np.maximum(dist, 1) / max_exact) / np.log(REL_MAX_DIST / max_exact)
                         * (REL_BUCKETS - max_exact)).astype(np.int32)
    large = np.minimum(large, REL_BUCKETS - 1)
    return np.where(dist < max_exact, dist, large).astype(np.int32)


def _attn_bias(rel_bias):
    qi = np.arange(ATTN_SPAN)[:, None]
    kj = np.arange(2 * ATTN_SPAN)[None, :]
    delta = qi - kj + ATTN_SPAN
    in_band = (delta >= 0) & (delta <= ATTN_SPAN)
    mats = []
    for gi, (_, dil) in enumerate(ATTN_CONFIGS):
        tab = rel_bias[:, gi * ATTN_GROUP_HEADS:(gi + 1) * ATTN_GROUP_HEADS].astype(F32)
        bucket = jnp.asarray(np.where(in_band, _t5_bucket(delta * dil), -1))
        onehot = (bucket[..., None] == jnp.arange(REL_BUCKETS)).astype(F32)
        b = jnp.einsum('qkb,bh->hqk', onehot, tab, precision=lax.Precision.HIGHEST)
        mats.append(jnp.where(jnp.asarray(in_band), b, NEG_BIG))
    bias = jnp.concatenate(mats, axis=0)
    prev_cols = jnp.asarray(kj < ATTN_SPAN)
    return jnp.concatenate([bias, jnp.where(prev_cols, NEG_BIG, bias)], axis=0)


def _pad_groups(a, axis):
    shp = a.shape
    a = a.reshape(shp[:axis] + (POOL_GROUPS, POOL_GDIM) + shp[axis + 1:])
    pad = [(0, 0)] * a.ndim
    pad[axis + 1] = (0, POOL_GPAD - POOL_GDIM)
    a = jnp.pad(a, pad)
    return a.reshape(shp[:axis] + (POOL_PAD,) + shp[axis + 1:])


def kernel(x, ffn1_w13, ffn1_w2, ln1_g, ln1_b, w_in, gate_b, pool_w, pool_b, pool_scale, conv_w, conv_b, dt_bias, a_log, d_skip, ssd_norm, rel_bias, p_pool, p_ssd, p_attn, w_out, ln2_g, ln2_b, ffn2_w13, ffn2_w2, ln3_g, ln3_b):
    sec = [w_in[:, :, IN_OFFS[i]:IN_OFFS[i + 1]] for i in range(len(IN_SIZES))]
    w_u, w_z, w_xbc, w_dt, w_q, w_k, w_v, w_g = sec
    w_pool = _pad_groups(w_u, 2).astype(BF16)
    w_z = w_z.astype(BF16)
    w_xbc = w_xbc.astype(BF16)
    w_dt = jnp.pad(w_dt, ((0, 0), (0, 0), (0, DT_PAD - SSD_HEADS))).astype(BF16)
    w_qkv = jnp.concatenate(
        [s[:, :, gi * ATTN_OUT:(gi + 1) * ATTN_OUT] for gi in range(len(ATTN_CONFIGS)) for s in (w_q, w_k, w_v)],
        axis=2).astype(BF16)
    w_g = w_g.astype(BF16)

    pw = jnp.pad(pool_w, ((0, 0), (0, 0), (0, POOL_GPAD - POOL_GDIM), (0, POOL_GPAD - POOL_GDIM))).astype(BF16)
    pb = jnp.pad(pool_b, ((0, 0), (0, 0), (0, POOL_GPAD - POOL_GDIM)))
    ps = jnp.pad(pool_scale.reshape(DEPTH, POOL_GROUPS, POOL_GDIM), ((0, 0), (0, 0), (0, POOL_GPAD - POOL_GDIM)))
    p_pool_b = _pad_groups(p_pool, 1).astype(BF16)
    p_ssd_b = p_ssd.astype(BF16)
    p_attn_b = p_attn.astype(BF16)
    w_out_b = w_out.astype(BF16)

    dtb = jnp.pad(dt_bias, ((0, 0), (0, DT_PAD - SSD_HEADS)))[:, None, :]
    alog = jnp.pad(a_log, ((0, 0), (0, DT_PAD - SSD_HEADS)))[:, None, :]
    dsk = jnp.repeat(d_skip, SSD_HEADDIM, axis=1)[:, None, :]
    cb = conv_b[:, None, :]
    nw = ssd_norm[:, None, :]
    bias = _attn_bias(rel_bias)

    f1_w13, f1_w2 = ffn1_w13.astype(BF16), ffn1_w2.astype(BF16)
    f2_w13, f2_w2 = ffn2_w13.astype(BF16), ffn2_w2.astype(BF16)
    vec = lambda a: a[:, None, :]

    h = x.reshape(N_TOK, D_MODEL)
    for i in range(DEPTH):
        h, hb = _ffn(h, f1_w13, f1_w2, vec(ln1_g), vec(ln1_b), i, True)
        ya = _pool(hb, w_pool, pw, pb, ps, i)
        yb = _ssd(hb, w_z, w_xbc, w_dt, conv_w, cb, dtb, alog, dsk, nw, i)
        yc = _attn(hb, w_qkv, bias, i)
        h = _merge(h, ya, yb, yc, w_g, gate_b, p_pool_b, p_ssd_b, p_attn_b, w_out_b, vec(ln2_g), vec(ln2_b), i)
        (h,) = _ffn(h, f2_w13, f2_w2, vec(ln3_g), vec(ln3_b), i, False)
    return h.reshape(BATCH, SEQ, D_MODEL)
```

```python
import functools

import numpy as np
import jax
import jax.numpy as jnp
from jax import lax
from jax.experimental import pallas as pl
from jax.experimental.pallas import tpu as pltpu

F32 = jnp.float32
BF16 = jnp.bfloat16

D_MODEL = 1024
BATCH = 16
SEQ = 2048
DEPTH = 4
N_TOK = BATCH * SEQ
LN_EPS = 1e-5
DN_ALPHA = (2.0 * DEPTH) ** 0.25
FFN_RES = 0.5
D_FF = ((8 * D_MODEL // 3 + 127) // 128) * 128
POOL_WIDTH = 3 * D_MODEL // 4
POOL_WINDOWS = (2, 4, 8, 16)
POOL_GROUPS = len(POOL_WINDOWS)
POOL_GDIM = POOL_WIDTH // POOL_GROUPS
SSD_INNER = D_MODEL
SSD_HEADDIM = 64
SSD_HEADS = SSD_INNER // SSD_HEADDIM
SSD_GROUPS = 4
SSD_HPG = SSD_HEADS // SSD_GROUPS
SSD_STATE = 128
SSD_CONV = 4
SSD_CHUNK = 128
SSD_CONV_CH = SSD_INNER + 2 * SSD_GROUPS * SSD_STATE
SSD_EPS = 1e-5
ATTN_CONFIGS = ((128, 1), (512, 4), (2048, 16))
ATTN_HEAD_DIM = 64
ATTN_GROUP_HEADS = 4
ATTN_HEADS = ATTN_GROUP_HEADS * len(ATTN_CONFIGS)
ATTN_WIDTH = ATTN_HEADS * ATTN_HEAD_DIM
ATTN_OUT = ATTN_GROUP_HEADS * ATTN_HEAD_DIM
ATTN_SPAN = 128
REL_BUCKETS = 32
REL_MAX_DIST = 2048
N_BRANCH = 3
IN_SIZES = (POOL_WIDTH, SSD_INNER, SSD_CONV_CH, SSD_HEADS, ATTN_WIDTH, ATTN_WIDTH, ATTN_WIDTH, N_BRANCH * D_MODEL)
IN_OFFS = tuple(sum(IN_SIZES[:i]) for i in range(len(IN_SIZES) + 1))

LANES = 128
MXU_DIM = 256
VMEM_LIMIT = 56 * 1024 * 1024

TM = 512
FFN_TM = 1024
FFN_CHUNK = MXU_DIM
POOL_GPAD = MXU_DIM
POOL_PAD = POOL_GROUPS * POOL_GPAD
POOL_HALO = 16
POOL_ROWS = 512
DT_PAD = LANES
CONV_HALO = 8
SSD_TS = 512
CONV_STRIDE = 4
CONV_UNIT = 8 * CONV_STRIDE
NEG_BIG = -1e30
ATTN_UNROLL = 16
ATTN_ORDER = (2, 1, 0)
assert ATTN_CONFIGS[ATTN_ORDER[-1]][1] == 1

assert all(w // d == ATTN_SPAN for w, d in ATTN_CONFIGS)
assert POOL_HALO >= max(POOL_WINDOWS) and CONV_HALO >= SSD_CONV - 1

_NT = (((1,), (1,)), ((), ()))
_TN = (((0,), (0,)), ((), ()))


def _dot(a, b):
    return jnp.dot(a, b, preferred_element_type=F32)


def _silu(a):
    h = 0.5 * a
    return h + h * jnp.tanh(h)


def _sigmoid(a):
    return 0.5 + 0.5 * jnp.tanh(0.5 * a)


def _layer_norm(r, g, b):
    mu = jnp.mean(r, axis=-1, keepdims=True)
    c = r - mu
    var = jnp.mean(c * c, axis=-1, keepdims=True)
    return c * lax.rsqrt(var + LN_EPS) * g + b


def _layer_spec(block, layer):
    nd = len(block)
    return pl.BlockSpec((None,) + tuple(block), lambda *_: (layer,) + (0,) * nd,
                        pipeline_mode=pl.Buffered(1))


def _params(*sem):
    return pltpu.CompilerParams(dimension_semantics=sem, vmem_limit_bytes=VMEM_LIMIT)


def _ffn_body(x_ref, w13_ref, w2_ref, g_ref, b_ref, *rest, emit_bf16):
    if emit_bf16:
        o_ref, ob_ref, h_ref = rest
    else:
        o_ref, h_ref = rest
    for half in range(FFN_TM // TM):
        rows = slice(half * TM, (half + 1) * TM)
        x = x_ref[rows, :]
        xb = x.astype(BF16)
        for c in range(D_FF // FFN_CHUNK):
            lo = c * FFN_CHUNK
            a = _dot(xb, w13_ref[:, lo:lo + FFN_CHUNK])
            g = _dot(xb, w13_ref[:, D_FF + lo:D_FF + lo + FFN_CHUNK])
            h_ref[half, :, lo:lo + FFN_CHUNK] = (_silu(a) * g).astype(BF16)
        y = _dot(h_ref[half], w2_ref[...])
        out = _layer_norm(DN_ALPHA * x + FFN_RES * y, g_ref[...], b_ref[...])
        o_ref[rows, :] = out
        if emit_bf16:
            ob_ref[rows, :] = out.astype(BF16)


def _ffn(x, w13, w2, g, b, layer, emit_bf16):
    row = pl.BlockSpec((FFN_TM, D_MODEL), lambda t: (t, 0))
    out_shape = [jax.ShapeDtypeStruct((N_TOK, D_MODEL), F32)]
    out_specs = [row]
    if emit_bf16:
        out_shape.append(jax.ShapeDtypeStruct((N_TOK, D_MODEL), BF16))
        out_specs.append(row)
    return pl.pallas_call(
        functools.partial(_ffn_body, emit_bf16=emit_bf16),
        grid=(N_TOK // FFN_TM,),
        in_specs=[row,
                  _layer_spec((D_MODEL, 2 * D_FF), layer),
                  _layer_spec((D_FF, D_MODEL), layer),
                  _layer_spec((1, D_MODEL), layer),
                  _layer_spec((1, D_MODEL), layer)],
        out_specs=out_specs,
        out_shape=out_shape,
        scratch_shapes=[pltpu.VMEM((FFN_TM // TM, TM, D_FF), BF16)],
        compiler_params=_params("parallel"),
        name="ffn",
    )(x, w13, w2, g, b)


def _pool_body(xb_ref, w_ref, pw_ref, pb_ref, ps_ref, o_ref, u_ref):
    u_ref[0:POOL_HALO, :] = jnp.zeros((POOL_HALO, POOL_PAD), F32)

    def proj(j):
        u_ref[POOL_HALO + j * TM:POOL_HALO + (j + 1) * TM, :] = _dot(xb_ref[j * TM:(j + 1) * TM, :], w_ref[...])

    def chunk(c):
        r0 = c * POOL_ROWS
        t = r0 + lax.broadcasted_iota(jnp.int32, (POOL_ROWS, 1), 0)
        for g, win in enumerate(POOL_WINDOWS):
            cols = slice(g * POOL_GPAD, (g + 1) * POOL_GPAD)
            ug = u_ref[r0:r0 + POOL_ROWS + POOL_HALO, cols]
            s = ug + pltpu.roll(ug, 1, 0)
            k = 2
            while k < win:
                s = s + pltpu.roll(s, k, 0)
                k *= 2
            inv = 1.0 / jnp.minimum(t + 1, win).astype(F32)
            pooled = s[POOL_HALO:] * inv - ug[POOL_HALO:]
            y = _dot(pooled.astype(BF16), pw_ref[g])
            y = (y + pb_ref[g:g + 1, :]) * ps_ref[g:g + 1, :]
            o_ref[r0:r0 + POOL_ROWS, cols] = y.astype(BF16)

    per_tile = TM // POOL_ROWS
    for j in range(SEQ // TM + 1):
        if j < SEQ // TM:
            proj(j)
        if j > 0:
            for c in range((j - 1) * per_tile, j * per_tile):
                chunk(c)


def _pool(xb, w, pw, pb, ps, layer):
    seq = pl.BlockSpec((SEQ, D_MODEL), lambda b: (b, 0))
    return pl.pallas_call(
        _pool_body,
        grid=(BATCH,),
        in_specs=[seq,
                  _layer_spec((D_MODEL, POOL_PAD), layer),
                  _layer_spec((POOL_GROUPS, POOL_GPAD, POOL_GPAD), layer),
                  _layer_spec((POOL_GROUPS, POOL_GPAD), layer),
                  _layer_spec((POOL_GROUPS, POOL_GPAD), layer)],
        out_specs=pl.BlockSpec((SEQ, POOL_PAD), lambda b: (b, 0)),
        out_shape=jax.ShapeDtypeStruct((N_TOK, POOL_PAD), BF16),
        scratch_shapes=[pltpu.VMEM((POOL_HALO + SEQ, POOL_PAD), F32)],
        compiler_params=_params("parallel"),
        name="pool_mixer",
    )(xb, w, pw, pb, ps)


def _split3_bf16(v):
    hi = v.astype(BF16)
    r1 = v - hi.astype(F32)
    mid = r1.astype(BF16)
    lo = (r1 - mid.astype(F32)).astype(BF16)
    return hi, mid, lo


def _softplus(v):
    return jnp.maximum(v, 0.0) + jnp.log1p(jnp.exp(-jnp.abs(v)))


def _interleaved_time(i):
    return (i & ~(CONV_UNIT - 1)) | ((i & 7) << 2) | ((i >> 3) & (CONV_STRIDE - 1))


def _ssd_body(xb_ref, wz_ref, wx_ref, wdt_ref, ex_ref, cw_ref, cb_ref, dtb_ref, alog_ref, dsk_ref, nw_ref,
              o_ref, pad_ref, xc_ref, dt_ref, acum_ref, ex2_ref, y_ref, st_ref):
    n_ct = SSD_CONV_CH // LANES
    n_chunks = SSD_TS // SSD_CHUNK
    units = [(u, v) for u in range(SSD_CHUNK // CONV_UNIT) for v in range(CONV_STRIDE)]

    @pl.when(pl.program_id(1) == 0)
    def _():
        st_ref[...] = jnp.zeros(st_ref.shape, F32)
        for c in range(n_ct):
            pad_ref[c, 0:CONV_HALO, :] = jnp.zeros((CONV_HALO, LANES), F32)

    def interleaved(ref_rows, base):
        return jnp.concatenate([ref_rows(pl.ds(base + u * CONV_UNIT + v, 8, stride=CONV_STRIDE))
                                for u, v in units], axis=0)

    xb = xb_ref[...]
    raw = _dot(xb, wx_ref[...])
    for c in range(n_ct):
        pad_ref[c, CONV_HALO:CONV_HALO + SSD_TS, :] = raw[:, c * LANES:(c + 1) * LANES]

    def conv(k):
        for c in range(n_ct):
            cols = slice(c * LANES, (c + 1) * LANES)
            acc = None
            for j in range(SSD_CONV):
                tap = jnp.broadcast_to(cw_ref[j:j + 1, cols], (SSD_CHUNK, LANES))
                base = CONV_HALO + k * SSD_CHUNK - (SSD_CONV - 1) + j
                term = tap * interleaved(lambda rows, c=c: pad_ref[c, rows, :], base)
                acc = term if acc is None else acc + term
            xc_ref[pl.ds(k * SSD_CHUNK, SSD_CHUNK), cols] = _silu(acc + cb_ref[:, cols])

    for k in range(n_chunks):
        conv(k)
    for c in range(n_ct):
        pad_ref[c, 0:CONV_HALO, :] = pad_ref[c, SSD_TS:SSD_TS + CONV_HALO, :]

    li = _interleaved_time(lax.broadcasted_iota(jnp.int32, (SSD_CHUNK, SSD_CHUNK), 0))
    si = _interleaved_time(lax.broadcasted_iota(jnp.int32, (SSD_CHUNK, SSD_CHUNK), 1))
    tril = li >= si
    ones_tril = jnp.where(tril, 1.0, 0.0).astype(BF16)
    first_head = lax.broadcasted_iota(jnp.int32, (SSD_CHUNK, LANES), 1) < SSD_HEADDIM
    gw = SSD_HPG * SSD_HEADDIM

    dt_ref[...] = _softplus(_dot(xb, wdt_ref[...]) + dtb_ref[...])
    a_neg = -jnp.exp(alog_ref[...])
    dts, tails = [], []
    for k in range(n_chunks):
        dt_k = interleaved(lambda rows: dt_ref[rows, :], k * SSD_CHUNK)
        parts = _dot(ones_tril, jnp.concatenate(_split3_bf16(dt_k * a_neg), axis=1))
        acum_k = (parts[:, 0:DT_PAD] + parts[:, DT_PAD:2 * DT_PAD]) + parts[:, 2 * DT_PAD:3 * DT_PAD]
        acum_ref[pl.ds(k * SSD_CHUNK, SSD_CHUNK), :] = acum_k
        dts.append(dt_k)
        tails.append(acum_k[SSD_CHUNK - 1:SSD_CHUNK, :] - acum_k)
    trio = jnp.concatenate(dts + [acum_ref[...]] + tails, axis=0)
    hi = trio.astype(BF16)
    mid = (trio - hi.astype(F32)).astype(BF16)
    ex2_ref[...] = _dot(jnp.concatenate([hi, mid], axis=1), ex_ref[...])

    def chunk(k):
        rows = pl.ds(k * SSD_CHUNK, SSD_CHUNK)
        acum = acum_ref[rows, :]
        acum_t = acum.T
        acx = ex2_ref[pl.ds(SSD_TS + k * SSD_CHUNK, SSD_CHUNK), :]
        xs = xc_ref[rows, 0:SSD_INNER]
        xdt = xs * ex2_ref[rows, :]
        xw_b = (xdt * jnp.exp(ex2_ref[pl.ds(2 * SSD_TS + k * SSD_CHUNK, SSD_CHUNK), :])).astype(BF16)
        e_cum = jnp.exp(acx)
        e_last = e_cum[SSD_CHUNK - 1:SSD_CHUNK, :]
        for g in range(SSD_GROUPS):
            gc = slice(g * gw, (g + 1) * gw)
            b_lo = SSD_INNER + g * SSD_STATE
            c_lo = SSD_INNER + SSD_GROUPS * SSD_STATE + g * SSD_STATE
            bmb = xc_ref[rows, b_lo:b_lo + SSD_STATE].astype(BF16)
            cmb = xc_ref[rows, c_lo:c_lo + SSD_STATE].astype(BF16)
            cbm = lax.dot_general(cmb, bmb, _NT, preferred_element_type=F32)
            prev = st_ref[g]
            y_off = _dot(cmb, prev.astype(BF16)) * e_cum[:, gc]
            new = lax.dot_general(bmb, xw_b[:, gc], _TN, preferred_element_type=F32)
            st_ref[g] = prev * e_last[:, gc] + new
            for t in range(gw // LANES):
                tc = slice(g * gw + t * LANES, g * gw + (t + 1) * LANES)
                x_pair = xdt[:, tc]
                mix = []
                for e in range(LANES // SSD_HEADDIM):
                    h = g * SSD_HPG + t * (LANES // SSD_HEADDIM) + e
                    diff = acum[:, h:h + 1] - acum_t[h:h + 1, :]
                    decay = jnp.exp(jnp.where(tril, diff, -jnp.inf))
                    mix.append((cbm * decay).astype(BF16))
                x_split = jnp.concatenate([jnp.where(first_head, x_pair, 0.0).astype(BF16),
                                           jnp.where(first_head, 0.0, x_pair).astype(BF16)], axis=0)
                y_t = (_dot(jnp.concatenate(mix, axis=1), x_split) + y_off[:, t * LANES:(t + 1) * LANES]
                       + dsk_ref[:, tc] * xs[:, tc])
                for n, (u, v) in enumerate(units):
                    y_ref[g * (gw // LANES) + t,
                          pl.ds(k * SSD_CHUNK + u * CONV_UNIT + v, 8, stride=CONV_STRIDE), :] = y_t[n * 8:(n + 1) * 8, :]

    for k in range(n_chunks):
        chunk(k)

    z = _dot(xb, wz_ref[...])
    for g in range(SSD_GROUPS):
        cols = slice(g * gw, (g + 1) * gw)
        yg = jnp.concatenate([y_ref[g * (gw // LANES) + t] for t in range(gw // LANES)], axis=1)
        yg = yg * _silu(z[:, cols])
        ms = jnp.mean(yg * yg, axis=-1, keepdims=True)
        o_ref[:, cols] = (yg * lax.rsqrt(ms + SSD_EPS) * nw_ref[:, cols]).astype(BF16)


def _head_expander():
    e = (np.arange(SSD_INNER)[None, :] // SSD_HEADDIM == np.arange(DT_PAD)[:, None]).astype(np.float32)
    return jnp.asarray(np.concatenate([e, e], axis=0), BF16)


def _ssd(xb, wz, wx, wdt, cw, cb, dtb, alog, dsk, nw, layer):
    steps = SEQ // SSD_TS
    row = pl.BlockSpec((SSD_TS, D_MODEL), lambda b, s: (b * steps + s, 0))
    return pl.pallas_call(
        _ssd_body,
        grid=(BATCH, steps),
        in_specs=[row,
                  _layer_spec((D_MODEL, SSD_INNER), layer),
                  _layer_spec((D_MODEL, SSD_CONV_CH), layer),
                  _layer_spec((D_MODEL, DT_PAD), layer),
                  pl.BlockSpec((2 * DT_PAD, SSD_INNER), lambda b, s: (0, 0), pipeline_mode=pl.Buffered(1)),
                  _layer_spec((SSD_CONV, SSD_CONV_CH), layer),
                  _layer_spec((1, SSD_CONV_CH), layer),
                  _layer_spec((1, DT_PAD), layer),
                  _layer_spec((1, DT_PAD), layer),
                  _layer_spec((1, SSD_INNER), layer),
                  _layer_spec((1, SSD_INNER), layer)],
        out_specs=pl.BlockSpec((SSD_TS, SSD_INNER), lambda b, s: (b * steps + s, 0)),
        out_shape=jax.ShapeDtypeStruct((N_TOK, SSD_INNER), BF16),
        scratch_shapes=[pltpu.VMEM((SSD_CONV_CH // LANES, CONV_HALO + SSD_TS, LANES), F32),
                        pltpu.VMEM((SSD_TS, SSD_CONV_CH), F32),
                        pltpu.VMEM((SSD_TS, DT_PAD), F32),
                        pltpu.VMEM((SSD_TS, DT_PAD), F32),
                        pltpu.VMEM((3 * SSD_TS, SSD_INNER), F32),
                        pltpu.VMEM((SSD_INNER // LANES, SSD_TS, LANES), F32),
                        pltpu.VMEM((SSD_GROUPS, SSD_STATE, SSD_HPG * SSD_HEADDIM), F32)],
        compiler_params=_params("arbitrary", "arbitrary"),
        name="ssd_mixer",
    )(xb, wz, wx, wdt, _head_expander(), cw, cb, dtb, alog, dsk, nw)


def _attn_body(xb_ref, w_ref, bias_ref, o_ref, tile_ref, qh_ref, kb_ref, vb_ref, na_ref, ma_ref, da_ref):
    n_lt = ATTN_OUT // LANES
    hpt = LANES // ATTN_HEAD_DIM
    lane = lax.broadcasted_iota(jnp.int32, (ATTN_SPAN, LANES), 1)
    first_head = lane < ATTN_HEAD_DIM
    for t in range(n_lt):
        kb_ref[t, 0:ATTN_SPAN, :] = jnp.zeros((ATTN_SPAN, LANES), BF16)
        vb_ref[t, 0:ATTN_SPAN, 0:LANES] = jnp.zeros((ATTN_SPAN, LANES), BF16)
        vb_ref[t, :, LANES:2 * LANES] = jnp.ones((ATTN_SPAN + SEQ, LANES), BF16)

    for g in ATTN_ORDER:
        dil = ATTN_CONFIGS[g][1]
        sub = SEQ // dil
        n_blk = sub // ATTN_SPAN
        per = TM // dil
        wcol = g * 3 * ATTN_OUT
        first_rows = lax.broadcasted_iota(jnp.int32, (per, LANES), 1) < ATTN_HEAD_DIM

        for j in range(SEQ // TM):
            qkv = _dot(xb_ref[j * TM:(j + 1) * TM, :], w_ref[:, wcol:wcol + 3 * ATTN_OUT])
            for which in range(3):
                for t in range(n_lt):
                    c0 = which * ATTN_OUT + t * LANES
                    piece = qkv[:, c0:c0 + LANES]
                    if which == 0:
                        piece = piece * (ATTN_HEAD_DIM ** -0.5)
                    slot = (j * 3 + which) * n_lt + t
                    if dil > 1:
                        tile_ref[slot] = piece
                    for r in range(dil):
                        if dil > 1:
                            piece = tile_ref[slot, pl.ds(r, per, stride=dil), :]
                        if which == 0:
                            dst = pl.ds(r * sub + j * per, per)
                            qh_ref[t * hpt, dst, :] = jnp.where(first_rows, piece, 0.0).astype(BF16)
                            qh_ref[t * hpt + 1, dst, :] = jnp.where(first_rows, 0.0, piece).astype(BF16)
                        elif which == 1:
                            kb_ref[t, pl.ds(ATTN_SPAN + r * sub + j * per, per), :] = piece.astype(BF16)
                        else:
                            vb_ref[t, pl.ds(ATTN_SPAN + r * sub + j * per, per), 0:LANES] = piece.astype(BF16)

        def block(u, carry, g=g, n_blk=n_blk, dil=dil):
            r0 = pl.multiple_of(u * ATTN_SPAN, ATTN_SPAN)
            rows = pl.ds(r0, ATTN_SPAN)
            krows = pl.ds(r0, 2 * ATTN_SPAN)
            no_prev = ATTN_HEADS if n_blk == 1 else jnp.where(u % n_blk == 0, ATTN_HEADS, 0)
            nat = rows if dil == 1 else pl.ds(u // n_blk + (u % n_blk) * (ATTN_SPAN * dil), ATTN_SPAN, stride=dil)
            for t in range(n_lt):
                k2 = kb_ref[t, krows, :]
                v2 = vb_ref[t, krows, :]
                h0 = g * ATTN_GROUP_HEADS + t * hpt
                q2 = jnp.concatenate([qh_ref[t * hpt + e, rows, :] for e in range(hpt)], axis=0)
                s = (lax.dot_general(q2, k2, _NT, preferred_element_type=F32)
                     + jnp.concatenate([bias_ref[h0 + e + no_prev] for e in range(hpt)], axis=0))
                m = jnp.max(s, axis=-1, keepdims=True)
                pv = _dot(jnp.exp(s - m).astype(BF16), v2)
                mb = jnp.broadcast_to(m, (hpt * ATTN_SPAN, LANES))
                parts = [(pv[e * ATTN_SPAN:(e + 1) * ATTN_SPAN, 0:LANES], mb[e * ATTN_SPAN:(e + 1) * ATTN_SPAN],
                          pv[e * ATTN_SPAN:(e + 1) * ATTN_SPAN, LANES:2 * LANES]) for e in range(hpt)]
                n_g, m_g, d_g = (jnp.where(first_head, a, b) for a, b in zip(parts[0], parts[1]))
                if g == ATTN_ORDER[0]:
                    na_ref[t, nat, :] = n_g
                    ma_ref[t, nat, :] = m_g
                    da_ref[t, nat, :] = d_g
                else:
                    m_a = ma_ref[t, nat, :]
                    m_n = jnp.maximum(m_a, m_g)
                    e_a = jnp.exp(m_a - m_n)
                    e_g = jnp.exp(m_g - m_n)
                    n_n = na_ref[t, nat, :] * e_a + n_g * e_g
                    d_n = da_ref[t, nat, :] * e_a + d_g * e_g
                    if g == ATTN_ORDER[-1]:
                        o_ref[rows, t * LANES:(t + 1) * LANES] = (n_n / d_n).astype(BF16)
                    else:
                        na_ref[t, nat, :] = n_n
                        da_ref[t, nat, :] = d_n
                        ma_ref[t, nat, :] = m_n
            return carry

        lax.fori_loop(0, SEQ // ATTN_SPAN, block, 0, unroll=ATTN_UNROLL)


def _attn(xb, w, bias, layer):
    n_lt = ATTN_OUT // LANES
    return pl.pallas_call(
        _attn_body,
        grid=(BATCH,),
        in_specs=[pl.BlockSpec((SEQ, D_MODEL), lambda b: (b, 0)),
                  _layer_spec((D_MODEL, 3 * ATTN_WIDTH), layer),
                  pl.BlockSpec((2 * ATTN_HEADS, ATTN_SPAN, 2 * ATTN_SPAN), lambda b: (0, 0, 0),
                               pipeline_mode=pl.Buffered(1))],
        out_specs=pl.BlockSpec((SEQ, ATTN_OUT), lambda b: (b, 0)),
        out_shape=jax.ShapeDtypeStruct((N_TOK, ATTN_OUT), BF16),
        scratch_shapes=[pltpu.VMEM((SEQ // TM * 3 * n_lt, TM, LANES), F32),
                        pltpu.VMEM((ATTN_GROUP_HEADS, SEQ, LANES), BF16),
                        pltpu.VMEM((n_lt, ATTN_SPAN + SEQ, LANES), BF16),
                        pltpu.VMEM((n_lt, ATTN_SPAN + SEQ, 2 * LANES), BF16)]
                       + [pltpu.VMEM((n_lt, SEQ, LANES), F32)] * 3,
        compiler_params=_params("parallel"),
        name="dilated_attn",
    )(xb, w, bias)


def _merge_body(x_ref, ya_ref, yb_ref, yc_ref, wg_ref, gb_ref, pa_ref, pb_ref, pc_ref, wo_ref, g_ref, b_ref, o_ref,
                m_ref):
    for half in range(FFN_TM // TM):
        rows = slice(half * TM, (half + 1) * TM)
        x = x_ref[rows, :]
        xb = x.astype(BF16)
        for c in range(D_MODEL // MXU_DIM):
            cs = slice(c * MXU_DIM, (c + 1) * MXU_DIM)
            acc = None
            for i, (y_ref, p_ref) in enumerate(((ya_ref, pa_ref), (yb_ref, pb_ref), (yc_ref, pc_ref))):
                gs = slice(i * D_MODEL + c * MXU_DIM, i * D_MODEL + (c + 1) * MXU_DIM)
                gate = _sigmoid(_dot(xb, wg_ref[:, gs]) + gb_ref[i:i + 1, cs])
                term = gate * _dot(y_ref[rows, :], p_ref[:, cs])
                acc = term if acc is None else acc + term
            m_ref[half, :, cs] = acc.astype(BF16)
        mix = _dot(m_ref[half], wo_ref[...])
        o_ref[rows, :] = _layer_norm(DN_ALPHA * x + mix, g_ref[...], b_ref[...])


def _merge(x, ya, yb, yc, wg, gb, pa, pb, pc, wo, g, b, layer):
    def row(width):
        return pl.BlockSpec((FFN_TM, width), lambda t: (t, 0))

    return pl.pallas_call(
        _merge_body,
        grid=(N_TOK // FFN_TM,),
        in_specs=[row(D_MODEL), row(POOL_PAD), row(SSD_INNER), row(ATTN_OUT),
                  _layer_spec((D_MODEL, N_BRANCH * D_MODEL), layer),
                  _layer_spec((N_BRANCH, D_MODEL), layer),
                  _layer_spec((POOL_PAD, D_MODEL), layer),
                  _layer_spec((SSD_INNER, D_MODEL), layer),
                  _layer_spec((ATTN_OUT, D_MODEL), layer),
                  _layer_spec((D_MODEL, D_MODEL), layer),
                  _layer_spec((1, D_MODEL), layer),
                  _layer_spec((1, D_MODEL), layer)],
        out_specs=row(D_MODEL),
        out_shape=jax.ShapeDtypeStruct((N_TOK, D_MODEL), F32),
        scratch_shapes=[pltpu.VMEM((FFN_TM // TM, TM, D_MODEL), BF16)],
        compiler_params=_params("parallel"),
        name="gated_merge",
    )(x, ya, yb, yc, wg, gb, pa, pb, pc, wo, g, b)


def _t5_bucket(dist):
    dist = np.maximum(dist, 0)
    max_exact = REL_BUCKETS // 2
    large = max_exact + (np.log(np.maximum(dist, 1) / max_exact) / np.log(REL_MAX_DIST / max_exact)
                         * (REL_BUCKETS - max_exact)).astype(np.int32)
    large = np.minimum(large, REL_BUCKETS - 1)
    return np.where(dist < max_exact, dist, large).astype(np.int32)


def _attn_bias(rel_bias):
    qi = np.arange(ATTN_SPAN)[:, None]
    kj = np.arange(2 * ATTN_SPAN)[None, :]
    delta = qi - kj + ATTN_SPAN
    in_band = (delta >= 0) & (delta <= ATTN_SPAN)
    mats = []
    for gi, (_, dil) in enumerate(ATTN_CONFIGS):
        tab = rel_bias[:, gi * ATTN_GROUP_HEADS:(gi + 1) * ATTN_GROUP_HEADS].astype(F32)
        bucket = jnp.asarray(np.where(in_band, _t5_bucket(delta * dil), -1))
        onehot = (bucket[..., None] == jnp.arange(REL_BUCKETS)).astype(F32)
        b = jnp.einsum('qkb,bh->hqk', onehot, tab, precision=lax.Precision.HIGHEST)
        mats.append(jnp.where(jnp.asarray(in_band), b, NEG_BIG))
    bias = jnp.concatenate(mats, axis=0)
    prev_cols = jnp.asarray(kj < ATTN_SPAN)
    return jnp.concatenate([bias, jnp.where(prev_cols, NEG_BIG, bias)], axis=0)


def _pad_groups(a, axis):
    shp = a.shape
    a = a.reshape(shp[:axis] + (POOL_GROUPS, POOL_GDIM) + shp[axis + 1:])
    pad = [(0, 0)] * a.ndim
    pad[axis + 1] = (0, POOL_GPAD - POOL_GDIM)
    a = jnp.pad(a, pad)
    return a.reshape(shp[:axis] + (POOL_PAD,) + shp[axis + 1:])


def kernel(x, ffn1_w13, ffn1_w2, ln1_g, ln1_b, w_in, gate_b, pool_w, pool_b, pool_scale, conv_w, conv_b, dt_bias, a_log, d_skip, ssd_norm, rel_bias, p_pool, p_ssd, p_attn, w_out, ln2_g, ln2_b, ffn2_w13, ffn2_w2, ln3_g, ln3_b):
    sec = [w_in[:, :, IN_OFFS[i]:IN_OFFS[i + 1]] for i in range(len(IN_SIZES))]
    w_u, w_z, w_xbc, w_dt, w_q, w_k, w_v, w_g = sec
    w_pool = _pad_groups(w_u, 2).astype(BF16)
    w_z = w_z.astype(BF16)
    w_xbc = w_xbc.astype(BF16)
    w_dt = jnp.pad(w_dt, ((0, 0), (0, 0), (0, DT_PAD - SSD_HEADS))).astype(BF16)
    w_qkv = jnp.concatenate(
        [s[:, :, gi * ATTN_OUT:(gi + 1) * ATTN_OUT] for gi in range(len(ATTN_CONFIGS)) for s in (w_q, w_k, w_v)],
        axis=2).astype(BF16)
    w_g = w_g.astype(BF16)

    pw = jnp.pad(pool_w, ((0, 0), (0, 0), (0, POOL_GPAD - POOL_GDIM), (0, POOL_GPAD - POOL_GDIM))).astype(BF16)
    pb = jnp.pad(pool_b, ((0, 0), (0, 0), (0, POOL_GPAD - POOL_GDIM)))
    ps = jnp.pad(pool_scale.reshape(DEPTH, POOL_GROUPS, POOL_GDIM), ((0, 0), (0, 0), (0, POOL_GPAD - POOL_GDIM)))
    p_pool_b = _pad_groups(p_pool, 1).astype(BF16)
    p_ssd_b = p_ssd.astype(BF16)
    p_attn_b = p_attn.astype(BF16)
    w_out_b = w_out.astype(BF16)

    dtb = jnp.pad(dt_bias, ((0, 0), (0, DT_PAD - SSD_HEADS)))[:, None, :]
    alog = jnp.pad(a_log, ((0, 0), (0, DT_PAD - SSD_HEADS)))[:, None, :]
    dsk = jnp.repeat(d_skip, SSD_HEADDIM, axis=1)[:, None, :]
    cb = conv_b[:, None, :]
    nw = ssd_norm[:, None, :]
    bias = _attn_bias(rel_bias)

    f1_w13, f1_w2 = ffn1_w13.astype(BF16), ffn1_w2.astype(BF16)
    f2_w13, f2_w2 = ffn2_w13.astype(BF16), ffn2_w2.astype(BF16)
    vec = lambda a: a[:, None, :]

    h = x.reshape(N_TOK, D_MODEL)
    for i in range(DEPTH):
        h, hb = _ffn(h, f1_w13, f1_w2, vec(ln1_g), vec(ln1_b), i, True)
        ya = _pool(hb, w_pool, pw, pb, ps, i)
        yb = _ssd(hb, w_z, w_xbc, w_dt, conv_w, cb, dtb, alog, dsk, nw, i)
        yc = _attn(hb, w_qkv, bias, i)
        h = _merge(h, ya, yb, yc, w_g, gate_b, p_pool_b, p_ssd_b, p_attn_b, w_out_b, vec(ln2_g), vec(ln2_b), i)
        (h,) = _ffn(h, f2_w13, f2_w2, vec(ln3_g), vec(ln3_b), i, False)
    return h.reshape(BATCH, SEQ, D_MODEL)
```

```python
import functools

import numpy as np
import jax
import jax.numpy as jnp
from jax import lax
from jax.experimental import pallas as pl
from jax.experimental.pallas import tpu as pltpu

F32 = jnp.float32
BF16 = jnp.bfloat16

D_MODEL = 1024
BATCH = 16
SEQ = 2048
DEPTH = 4
N_TOK = BATCH * SEQ
LN_EPS = 1e-5
DN_ALPHA = (2.0 * DEPTH) ** 0.25
FFN_RES = 0.5
D_FF = ((8 * D_MODEL // 3 + 127) // 128) * 128
POOL_WIDTH = 3 * D_MODEL // 4
POOL_WINDOWS = (2, 4, 8, 16)
POOL_GROUPS = len(POOL_WINDOWS)
POOL_GDIM = POOL_WIDTH // POOL_GROUPS
SSD_INNER = D_MODEL
SSD_HEADDIM = 64
SSD_HEADS = SSD_INNER // SSD_HEADDIM
SSD_GROUPS = 4
SSD_HPG = SSD_HEADS // SSD_GROUPS
SSD_STATE = 128
SSD_CONV = 4
SSD_CHUNK = 128
SSD_CONV_CH = SSD_INNER + 2 * SSD_GROUPS * SSD_STATE
SSD_EPS = 1e-5
ATTN_CONFIGS = ((128, 1), (512, 4), (2048, 16))
ATTN_HEAD_DIM = 64
ATTN_GROUP_HEADS = 4
ATTN_HEADS = ATTN_GROUP_HEADS * len(ATTN_CONFIGS)
ATTN_WIDTH = ATTN_HEADS * ATTN_HEAD_DIM
ATTN_OUT = ATTN_GROUP_HEADS * ATTN_HEAD_DIM
ATTN_SPAN = 128
REL_BUCKETS = 32
REL_MAX_DIST = 2048
N_BRANCH = 3
IN_SIZES = (POOL_WIDTH, SSD_INNER, SSD_CONV_CH, SSD_HEADS, ATTN_WIDTH, ATTN_WIDTH, ATTN_WIDTH, N_BRANCH * D_MODEL)
IN_OFFS = tuple(sum(IN_SIZES[:i]) for i in range(len(IN_SIZES) + 1))

LANES = 128
MXU_DIM = 256
VMEM_LIMIT = 56 * 1024 * 1024

TM = 512
FFN_TM = 1024
FFN_CHUNK = MXU_DIM
POOL_GPAD = MXU_DIM
POOL_PAD = POOL_GROUPS * POOL_GPAD
POOL_HALO = 16
POOL_ROWS = 512
DT_PAD = LANES
CONV_HALO = 8
SSD_TS = 512
CONV_STRIDE = 4
CONV_UNIT = 8 * CONV_STRIDE
NEG_BIG = -1e30
ATTN_UNROLL = 16
ATTN_ORDER = (2, 1, 0)
assert ATTN_CONFIGS[ATTN_ORDER[-1]][1] == 1

assert all(w // d == ATTN_SPAN for w, d in ATTN_CONFIGS)
assert POOL_HALO >= max(POOL_WINDOWS) and CONV_HALO >= SSD_CONV - 1

_NT = (((1,), (1,)), ((), ()))
_TN = (((0,), (0,)), ((), ()))


def _dot(a, b):
    return jnp.dot(a, b, preferred_element_type=F32)


def _silu(a):
    h = 0.5 * a
    return h + h * jnp.tanh(h)


def _sigmoid(a):
    return 0.5 + 0.5 * jnp.tanh(0.5 * a)


def _layer_norm(r, g, b):
    mu = jnp.mean(r, axis=-1, keepdims=True)
    c = r - mu
    var = jnp.mean(c * c, axis=-1, keepdims=True)
    return c * lax.rsqrt(var + LN_EPS) * g + b


def _layer_spec(block, layer):
    nd = len(block)
    return pl.BlockSpec((None,) + tuple(block), lambda *_: (layer,) + (0,) * nd,
                        pipeline_mode=pl.Buffered(1))


def _params(*sem):
    return pltpu.CompilerParams(dimension_semantics=sem, vmem_limit_bytes=VMEM_LIMIT)


def _ffn_body(x_ref, w13_ref, w2_ref, g_ref, b_ref, *rest, emit_bf16):
    if emit_bf16:
        o_ref, ob_ref, h_ref = rest
    else:
        o_ref, h_ref = rest
    for half in range(FFN_TM // TM):
        rows = slice(half * TM, (half + 1) * TM)
        x = x_ref[rows, :]
        xb = x.astype(BF16)
        for c in range(D_FF // FFN_CHUNK):
            lo = c * FFN_CHUNK
            a = _dot(xb, w13_ref[:, lo:lo + FFN_CHUNK])
            g = _dot(xb, w13_ref[:, D_FF + lo:D_FF + lo + FFN_CHUNK])
            h_ref[half, :, lo:lo + FFN_CHUNK] = (_silu(a) * g).astype(BF16)
        y = _dot(h_ref[half], w2_ref[...])
        out = _layer_norm(DN_ALPHA * x + FFN_RES * y, g_ref[...], b_ref[...])
        o_ref[rows, :] = out
        if emit_bf16:
            ob_ref[rows, :] = out.astype(BF16)


def _ffn(x, w13, w2, g, b, layer, emit_bf16):
    row = pl.BlockSpec((FFN_TM, D_MODEL), lambda t: (t, 0))
    out_shape = [jax.ShapeDtypeStruct((N_TOK, D_MODEL), F32)]
    out_specs = [row]
    if emit_bf16:
        out_shape.append(jax.ShapeDtypeStruct((N_TOK, D_MODEL), BF16))
        out_specs.append(row)
    return pl.pallas_call(
        functools.partial(_ffn_body, emit_bf16=emit_bf16),
        grid=(N_TOK // FFN_TM,),
        in_specs=[row,
                  _layer_spec((D_MODEL, 2 * D_FF), layer),
                  _layer_spec((D_FF, D_MODEL), layer),
                  _layer_spec((1, D_MODEL), layer),
                  _layer_spec((1, D_MODEL), layer)],
        out_specs=out_specs,
        out_shape=out_shape,
        scratch_shapes=[pltpu.VMEM((FFN_TM // TM, TM, D_FF), BF16)],
        compiler_params=_params("parallel"),
        name="ffn",
    )(x, w13, w2, g, b)


def _pool_body(xb_ref, w_ref, pw_ref, pb_ref, ps_ref, o_ref, u_ref):
    u_ref[0:POOL_HALO, :] = jnp.zeros((POOL_HALO, POOL_PAD), F32)

    def proj(j):
        u_ref[POOL_HALO + j * TM:POOL_HALO + (j + 1) * TM, :] = _dot(xb_ref[j * TM:(j + 1) * TM, :], w_ref[...])

    def chunk(c):
        r0 = c * POOL_ROWS
        t = r0 + lax.broadcasted_iota(jnp.int32, (POOL_ROWS, 1), 0)
        for g, win in enumerate(POOL_WINDOWS):
            cols = slice(g * POOL_GPAD, (g + 1) * POOL_GPAD)
            ug = u_ref[r0:r0 + POOL_ROWS + POOL_HALO, cols]
            s = ug + pltpu.roll(ug, 1, 0)
            k = 2
            while k < win:
                s = s + pltpu.roll(s, k, 0)
                k *= 2
            inv = 1.0 / jnp.minimum(t + 1, win).astype(F32)
            pooled = s[POOL_HALO:] * inv - ug[POOL_HALO:]
            y = _dot(pooled.astype(BF16), pw_ref[g])
            y = (y + pb_ref[g:g + 1, :]) * ps_ref[g:g + 1, :]
            o_ref[r0:r0 + POOL_ROWS, g * POOL_GDIM:(g + 1) * POOL_GDIM] = y[:, 0:POOL_GDIM].astype(BF16)

    per_tile = TM // POOL_ROWS
    for j in range(SEQ // TM + 1):
        if j < SEQ // TM:
            proj(j)
        if j > 0:
            for c in range((j - 1) * per_tile, j * per_tile):
                chunk(c)


def _pool(xb, w, pw, pb, ps, layer):
    seq = pl.BlockSpec((SEQ, D_MODEL), lambda b: (b, 0))
    return pl.pallas_call(
        _pool_body,
        grid=(BATCH,),
        in_specs=[seq,
                  _layer_spec((D_MODEL, POOL_PAD), layer),
                  _layer_spec((POOL_GROUPS, POOL_GPAD, POOL_GPAD), layer),
                  _layer_spec((POOL_GROUPS, POOL_GPAD), layer),
                  _layer_spec((POOL_GROUPS, POOL_GPAD), layer)],
        out_specs=pl.BlockSpec((SEQ, POOL_WIDTH), lambda b: (b, 0)),
        out_shape=jax.ShapeDtypeStruct((N_TOK, POOL_WIDTH), BF16),
        scratch_shapes=[pltpu.VMEM((POOL_HALO + SEQ, POOL_PAD), F32)],
        compiler_params=_params("parallel"),
        name="pool_mixer",
    )(xb, w, pw, pb, ps)


def _split3_bf16(v):
    hi = v.astype(BF16)
    r1 = v - hi.astype(F32)
    mid = r1.astype(BF16)
    lo = (r1 - mid.astype(F32)).astype(BF16)
    return hi, mid, lo


def _softplus(v):
    return jnp.maximum(v, 0.0) + jnp.log1p(jnp.exp(-jnp.abs(v)))


def _interleaved_time(i):
    return (i & ~(CONV_UNIT - 1)) | ((i & 7) << 2) | ((i >> 3) & (CONV_STRIDE - 1))


def _ssd_body(xb_ref, wz_ref, wx_ref, wdt_ref, ex_ref, cw_ref, cb_ref, dtb_ref, alog_ref, dsk_ref, nw_ref,
              o_ref, pad_ref, xc_ref, dt_ref, acum_ref, ex2_ref, y_ref, st_ref):
    n_ct = SSD_CONV_CH // LANES
    n_chunks = SSD_TS // SSD_CHUNK
    units = [(u, v) for u in range(SSD_CHUNK // CONV_UNIT) for v in range(CONV_STRIDE)]

    @pl.when(pl.program_id(1) == 0)
    def _():
        st_ref[...] = jnp.zeros(st_ref.shape, F32)
        for c in range(n_ct):
            pad_ref[c, 0:CONV_HALO, :] = jnp.zeros((CONV_HALO, LANES), F32)

    def interleaved(ref_rows, base):
        return jnp.concatenate([ref_rows(pl.ds(base + u * CONV_UNIT + v, 8, stride=CONV_STRIDE))
                                for u, v in units], axis=0)

    xb = xb_ref[...]
    raw = _dot(xb, wx_ref[...])
    for c in range(n_ct):
        pad_ref[c, CONV_HALO:CONV_HALO + SSD_TS, :] = raw[:, c * LANES:(c + 1) * LANES]

    def conv(k):
        for c in range(n_ct):
            cols = slice(c * LANES, (c + 1) * LANES)
            acc = None
            for j in range(SSD_CONV):
                tap = jnp.broadcast_to(cw_ref[j:j + 1, cols], (SSD_CHUNK, LANES))
                base = CONV_HALO + k * SSD_CHUNK - (SSD_CONV - 1) + j
                term = tap * interleaved(lambda rows, c=c: pad_ref[c, rows, :], base)
                acc = term if acc is None else acc + term
            xc_ref[pl.ds(k * SSD_CHUNK, SSD_CHUNK), cols] = _silu(acc + cb_ref[:, cols])

    for k in range(n_chunks):
        conv(k)
    for c in range(n_ct):
        pad_ref[c, 0:CONV_HALO, :] = pad_ref[c, SSD_TS:SSD_TS + CONV_HALO, :]

    li = _interleaved_time(lax.broadcasted_iota(jnp.int32, (SSD_CHUNK, SSD_CHUNK), 0))
    si = _interleaved_time(lax.broadcasted_iota(jnp.int32, (SSD_CHUNK, SSD_CHUNK), 1))
    tril = li >= si
    ones_tril = jnp.where(tril, 1.0, 0.0).astype(BF16)
    first_head = lax.broadcasted_iota(jnp.int32, (SSD_CHUNK, LANES), 1) < SSD_HEADDIM
    gw = SSD_HPG * SSD_HEADDIM

    dt_ref[...] = _softplus(_dot(xb, wdt_ref[...]) + dtb_ref[...])
    a_neg = -jnp.exp(alog_ref[...])
    dts, tails = [], []
    for k in range(n_chunks):
        dt_k = interleaved(lambda rows: dt_ref[rows, :], k * SSD_CHUNK)
        parts = _dot(ones_tril, jnp.concatenate(_split3_bf16(dt_k * a_neg), axis=1))
        acum_k = (parts[:, 0:DT_PAD] + parts[:, DT_PAD:2 * DT_PAD]) + parts[:, 2 * DT_PAD:3 * DT_PAD]
        acum_ref[pl.ds(k * SSD_CHUNK, SSD_CHUNK), :] = acum_k
        dts.append(dt_k)
        tails.append(acum_k[SSD_CHUNK - 1:SSD_CHUNK, :] - acum_k)
    trio = jnp.concatenate(dts + [acum_ref[...]] + tails, axis=0)
    hi = trio.astype(BF16)
    mid = (trio - hi.astype(F32)).astype(BF16)
    ex2_ref[...] = _dot(jnp.concatenate([hi, mid], axis=1), ex_ref[...])

    def chunk(k):
        rows = pl.ds(k * SSD_CHUNK, SSD_CHUNK)
        acum = acum_ref[rows, :]
        acum_t = acum.T
        acx = ex2_ref[pl.ds(SSD_TS + k * SSD_CHUNK, SSD_CHUNK), :]
        xs = xc_ref[rows, 0:SSD_INNER]
        xdt = xs * ex2_ref[rows, :]
        xw_b = (xdt * jnp.exp(ex2_ref[pl.ds(2 * SSD_TS + k * SSD_CHUNK, SSD_CHUNK), :])).astype(BF16)
        e_cum = jnp.exp(acx)
        e_last = e_cum[SSD_CHUNK - 1:SSD_CHUNK, :]
        for g in range(SSD_GROUPS):
            gc = slice(g * gw, (g + 1) * gw)
            b_lo = SSD_INNER + g * SSD_STATE
            c_lo = SSD_INNER + SSD_GROUPS * SSD_STATE + g * SSD_STATE
            bmb = xc_ref[rows, b_lo:b_lo + SSD_STATE].astype(BF16)
            cmb = xc_ref[rows, c_lo:c_lo + SSD_STATE].astype(BF16)
            cbm = lax.dot_general(cmb, bmb, _NT, preferred_element_type=F32)
            prev = st_ref[g]
            y_off = _dot(cmb, prev.astype(BF16)) * e_cum[:, gc]
            new = lax.dot_general(bmb, xw_b[:, gc], _TN, preferred_element_type=F32)
            st_ref[g] = prev * e_last[:, gc] + new
            for t in range(gw // LANES):
                tc = slice(g * gw + t * LANES, g * gw + (t + 1) * LANES)
                x_pair = xdt[:, tc]
                mix = []
                for e in range(LANES // SSD_HEADDIM):
                    h = g * SSD_HPG + t * (LANES // SSD_HEADDIM) + e
                    diff = acum[:, h:h + 1] - acum_t[h:h + 1, :]
                    decay = jnp.exp(jnp.where(tril, diff, -jnp.inf))
                    mix.append((cbm * decay).astype(BF16))
                x_split = jnp.concatenate([jnp.where(first_head, x_pair, 0.0).astype(BF16),
                                           jnp.where(first_head, 0.0, x_pair).astype(BF16)], axis=0)
                y_t = (_dot(jnp.concatenate(mix, axis=1), x_split) + y_off[:, t * LANES:(t + 1) * LANES]
                       + dsk_ref[:, tc] * xs[:, tc])
                for n, (u, v) in enumerate(units):
                    y_ref[g * (gw // LANES) + t,
                          pl.ds(k * SSD_CHUNK + u * CONV_UNIT + v, 8, stride=CONV_STRIDE), :] = y_t[n * 8:(n + 1) * 8, :]

    for k in range(n_chunks):
        chunk(k)

    z = _dot(xb, wz_ref[...])
    for g in range(SSD_GROUPS):
        cols = slice(g * gw, (g + 1) * gw)
        yg = jnp.concatenate([y_ref[g * (gw // LANES) + t] for t in range(gw // LANES)], axis=1)
        yg = yg * _silu(z[:, cols])
        ms = jnp.mean(yg * yg, axis=-1, keepdims=True)
        o_ref[:, cols] = (yg * lax.rsqrt(ms + SSD_EPS) * nw_ref[:, cols]).astype(BF16)


def _head_expander():
    e = (np.arange(SSD_INNER)[None, :] // SSD_HEADDIM == np.arange(DT_PAD)[:, None]).astype(np.float32)
    return jnp.asarray(np.concatenate([e, e], axis=0), BF16)


def _ssd(xb, wz, wx, wdt, cw, cb, dtb, alog, dsk, nw, layer):
    steps = SEQ // SSD_TS
    row = pl.BlockSpec((SSD_TS, D_MODEL), lambda b, s: (b * steps + s, 0))
    return pl.pallas_call(
        _ssd_body,
        grid=(BATCH, steps),
        in_specs=[row,
                  _layer_spec((D_MODEL, SSD_INNER), layer),
                  _layer_spec((D_MODEL, SSD_CONV_CH), layer),
                  _layer_spec((D_MODEL, DT_PAD), layer),
                  pl.BlockSpec((2 * DT_PAD, SSD_INNER), lambda b, s: (0, 0), pipeline_mode=pl.Buffered(1)),
                  _layer_spec((SSD_CONV, SSD_CONV_CH), layer),
                  _layer_spec((1, SSD_CONV_CH), layer),
                  _layer_spec((1, DT_PAD), layer),
                  _layer_spec((1, DT_PAD), layer),
                  _layer_spec((1, SSD_INNER), layer),
                  _layer_spec((1, SSD_INNER), layer)],
        out_specs=pl.BlockSpec((SSD_TS, SSD_INNER), lambda b, s: (b * steps + s, 0)),
        out_shape=jax.ShapeDtypeStruct((N_TOK, SSD_INNER), BF16),
        scratch_shapes=[pltpu.VMEM((SSD_CONV_CH // LANES, CONV_HALO + SSD_TS, LANES), F32),
                        pltpu.VMEM((SSD_TS, SSD_CONV_CH), F32),
                        pltpu.VMEM((SSD_TS, DT_PAD), F32),
                        pltpu.VMEM((SSD_TS, DT_PAD), F32),
                        pltpu.VMEM((3 * SSD_TS, SSD_INNER), F32),
                        pltpu.VMEM((SSD_INNER // LANES, SSD_TS, LANES), F32),
                        pltpu.VMEM((SSD_GROUPS, SSD_STATE, SSD_HPG * SSD_HEADDIM), F32)],
        compiler_params=_params("arbitrary", "arbitrary"),
        name="ssd_mixer",
    )(xb, wz, wx, wdt, _head_expander(), cw, cb, dtb, alog, dsk, nw)


def _attn_body(xb_ref, w_ref, bias_ref, o_ref, tile_ref, qh_ref, kb_ref, vb_ref, na_ref, ma_ref, da_ref):
    n_lt = ATTN_OUT // LANES
    hpt = LANES // ATTN_HEAD_DIM
    lane = lax.broadcasted_iota(jnp.int32, (ATTN_SPAN, LANES), 1)
    first_head = lane < ATTN_HEAD_DIM
    for t in range(n_lt):
        kb_ref[t, 0:ATTN_SPAN, :] = jnp.zeros((ATTN_SPAN, LANES), BF16)
        vb_ref[t, 0:ATTN_SPAN, 0:LANES] = jnp.zeros((ATTN_SPAN, LANES), BF16)
        vb_ref[t, :, LANES:2 * LANES] = jnp.ones((ATTN_SPAN + SEQ, LANES), BF16)

    for g in ATTN_ORDER:
        dil = ATTN_CONFIGS[g][1]
        sub = SEQ // dil
        n_blk = sub // ATTN_SPAN
        per = TM // dil
        wcol = g * 3 * ATTN_OUT
        first_rows = lax.broadcasted_iota(jnp.int32, (per, LANES), 1) < ATTN_HEAD_DIM

        for j in range(SEQ // TM):
            qkv = _dot(xb_ref[j * TM:(j + 1) * TM, :], w_ref[:, wcol:wcol + 3 * ATTN_OUT])
            for which in range(3):
                for t in range(n_lt):
                    c0 = which * ATTN_OUT + t * LANES
                    piece = qkv[:, c0:c0 + LANES]
                    if which == 0:
                        piece = piece * (ATTN_HEAD_DIM ** -0.5)
                    slot = (j * 3 + which) * n_lt + t
                    if dil > 1:
                        tile_ref[slot] = piece
                    for r in range(dil):
                        if dil > 1:
                            piece = tile_ref[slot, pl.ds(r, per, stride=dil), :]
                        if which == 0:
                            dst = pl.ds(r * sub + j * per, per)
                            qh_ref[t * hpt, dst, :] = jnp.where(first_rows, piece, 0.0).astype(BF16)
                            qh_ref[t * hpt + 1, dst, :] = jnp.where(first_rows, 0.0, piece).astype(BF16)
                        elif which == 1:
                            kb_ref[t, pl.ds(ATTN_SPAN + r * sub + j * per, per), :] = piece.astype(BF16)
                        else:
                            vb_ref[t, pl.ds(ATTN_SPAN + r * sub + j * per, per), 0:LANES] = piece.astype(BF16)

        def block(u, carry, g=g, n_blk=n_blk, dil=dil):
            r0 = pl.multiple_of(u * ATTN_SPAN, ATTN_SPAN)
            rows = pl.ds(r0, ATTN_SPAN)
            krows = pl.ds(r0, 2 * ATTN_SPAN)
            no_prev = ATTN_HEADS if n_blk == 1 else jnp.where(u % n_blk == 0, ATTN_HEADS, 0)
            nat = rows if dil == 1 else pl.ds(u // n_blk + (u % n_blk) * (ATTN_SPAN * dil), ATTN_SPAN, stride=dil)
            for t in range(n_lt):
                k2 = kb_ref[t, krows, :]
                v2 = vb_ref[t, krows, :]
                h0 = g * ATTN_GROUP_HEADS + t * hpt
                q2 = jnp.concatenate([qh_ref[t * hpt + e, rows, :] for e in range(hpt)], axis=0)
                s = (lax.dot_general(q2, k2, _NT, preferred_element_type=F32)
                     + jnp.concatenate([bias_ref[h0 + e + no_prev] for e in range(hpt)], axis=0))
                m = jnp.max(s, axis=-1, keepdims=True)
                pv = _dot(jnp.exp(s - m).astype(BF16), v2)
                mb = jnp.broadcast_to(m, (hpt * ATTN_SPAN, LANES))
                parts = [(pv[e * ATTN_SPAN:(e + 1) * ATTN_SPAN, 0:LANES], mb[e * ATTN_SPAN:(e + 1) * ATTN_SPAN],
                          pv[e * ATTN_SPAN:(e + 1) * ATTN_SPAN, LANES:2 * LANES]) for e in range(hpt)]
                n_g, m_g, d_g = (jnp.where(first_head, a, b) for a, b in zip(parts[0], parts[1]))
                if g == ATTN_ORDER[0]:
                    na_ref[t, nat, :] = n_g
                    ma_ref[t, nat, :] = m_g
                    da_ref[t, nat, :] = d_g
                else:
                    m_a = ma_ref[t, nat, :]
                    m_n = jnp.maximum(m_a, m_g)
                    e_a = jnp.exp(m_a - m_n)
                    e_g = jnp.exp(m_g - m_n)
                    n_n = na_ref[t, nat, :] * e_a + n_g * e_g
                    d_n = da_ref[t, nat, :] * e_a + d_g * e_g
                    if g == ATTN_ORDER[-1]:
                        o_ref[rows, t * LANES:(t + 1) * LANES] = (n_n / d_n).astype(BF16)
                    else:
                        na_ref[t, nat, :] = n_n
                        da_ref[t, nat, :] = d_n
                        ma_ref[t, nat, :] = m_n
            return carry

        lax.fori_loop(0, SEQ // ATTN_SPAN, block, 0, unroll=ATTN_UNROLL)


def _attn(xb, w, bias, layer):
    n_lt = ATTN_OUT // LANES
    return pl.pallas_call(
        _attn_body,
        grid=(BATCH,),
        in_specs=[pl.BlockSpec((SEQ, D_MODEL), lambda b: (b, 0)),
                  _layer_spec((D_MODEL, 3 * ATTN_WIDTH), layer),
                  pl.BlockSpec((2 * ATTN_HEADS, ATTN_SPAN, 2 * ATTN_SPAN), lambda b: (0, 0, 0),
                               pipeline_mode=pl.Buffered(1))],
        out_specs=pl.BlockSpec((SEQ, ATTN_OUT), lambda b: (b, 0)),
        out_shape=jax.ShapeDtypeStruct((N_TOK, ATTN_OUT), BF16),
        scratch_shapes=[pltpu.VMEM((SEQ // TM * 3 * n_lt, TM, LANES), F32),
                        pltpu.VMEM((ATTN_GROUP_HEADS, SEQ, LANES), BF16),
                        pltpu.VMEM((n_lt, ATTN_SPAN + SEQ, LANES), BF16),
                        pltpu.VMEM((n_lt, ATTN_SPAN + SEQ, 2 * LANES), BF16)]
                       + [pltpu.VMEM((n_lt, SEQ, LANES), F32)] * 3,
        compiler_params=_params("parallel"),
        name="dilated_attn",
    )(xb, w, bias)


def _merge_body(x_ref, ya_ref, yb_ref, yc_ref, wg_ref, gb_ref, pa_ref, pb_ref, pc_ref, wo_ref, g_ref, b_ref, o_ref,
                m_ref):
    for half in range(FFN_TM // TM):
        rows = slice(half * TM, (half + 1) * TM)
        x = x_ref[rows, :]
        xb = x.astype(BF16)
        for c in range(D_MODEL // MXU_DIM):
            cs = slice(c * MXU_DIM, (c + 1) * MXU_DIM)
            acc = None
            for i, (y_ref, p_ref) in enumerate(((ya_ref, pa_ref), (yb_ref, pb_ref), (yc_ref, pc_ref))):
                gs = slice(i * D_MODEL + c * MXU_DIM, i * D_MODEL + (c + 1) * MXU_DIM)
                gate = _sigmoid(_dot(xb, wg_ref[:, gs]) + gb_ref[i:i + 1, cs])
                term = gate * _dot(y_ref[rows, :], p_ref[:, cs])
                acc = term if acc is None else acc + term
            m_ref[half, :, cs] = acc.astype(BF16)
        mix = _dot(m_ref[half], wo_ref[...])
        o_ref[rows, :] = _layer_norm(DN_ALPHA * x + mix, g_ref[...], b_ref[...])


def _merge(x, ya, yb, yc, wg, gb, pa, pb, pc, wo, g, b, layer):
    def row(width):
        return pl.BlockSpec((FFN_TM, width), lambda t: (t, 0))

    return pl.pallas_call(
        _merge_body,
        grid=(N_TOK // FFN_TM,),
        in_specs=[row(D_MODEL), row(POOL_WIDTH), row(SSD_INNER), row(ATTN_OUT),
                  _layer_spec((D_MODEL, N_BRANCH * D_MODEL), layer),
                  _layer_spec((N_BRANCH, D_MODEL), layer),
                  _layer_spec((POOL_WIDTH, D_MODEL), layer),
                  _layer_spec((SSD_INNER, D_MODEL), layer),
                  _layer_spec((ATTN_OUT, D_MODEL), layer),
                  _layer_spec((D_MODEL, D_MODEL), layer),
                  _layer_spec((1, D_MODEL), layer),
                  _layer_spec((1, D_MODEL), layer)],
        out_specs=row(D_MODEL),
        out_shape=jax.ShapeDtypeStruct((N_TOK, D_MODEL), F32),
        scratch_shapes=[pltpu.VMEM((FFN_TM // TM, TM, D_MODEL), BF16)],
        compiler_params=_params("parallel"),
        name="gated_merge",
    )(x, ya, yb, yc, wg, gb, pa, pb, pc, wo, g, b)


def _t5_bucket(dist):
    dist = np.maximum(dist, 0)
    max_exact = REL_BUCKETS // 2
    large = max_exact + (np.log(np.maximum(dist, 1) / max_exact) / np.log(REL_MAX_DIST / max_exact)
                         * (REL_BUCKETS - max_exact)).astype(np.int32)
    large = np.minimum(large, REL_BUCKETS - 1)
    return np.where(dist < max_exact, dist, large).astype(np.int32)


def _attn_bias(rel_bias):
    qi = np.arange(ATTN_SPAN)[:, None]
    kj = np.arange(2 * ATTN_SPAN)[None, :]
    delta = qi - kj + ATTN_SPAN
    in_band = (delta >= 0) & (delta <= ATTN_SPAN)
    mats = []
    for gi, (_, dil) in enumerate(ATTN_CONFIGS):
        tab = rel_bias[:, gi * ATTN_GROUP_HEADS:(gi + 1) * ATTN_GROUP_HEADS].astype(F32)
        bucket = jnp.asarray(np.where(in_band, _t5_bucket(delta * dil), -1))
        onehot = (bucket[..., None] == jnp.arange(REL_BUCKETS)).astype(F32)
        b = jnp.einsum('qkb,bh->hqk', onehot, tab, precision=lax.Precision.HIGHEST)
        mats.append(jnp.where(jnp.asarray(in_band), b, NEG_BIG))
    bias = jnp.concatenate(mats, axis=0)
    prev_cols = jnp.asarray(kj < ATTN_SPAN)
    return jnp.concatenate([bias, jnp.where(prev_cols, NEG_BIG, bias)], axis=0)


def _pad_groups(a, axis):
    shp = a.shape
    a = a.reshape(shp[:axis] + (POOL_GROUPS, POOL_GDIM) + shp[axis + 1:])
    pad = [(0, 0)] * a.ndim
    pad[axis + 1] = (0, POOL_GPAD - POOL_GDIM)
    a = jnp.pad(a, pad)
    return a.reshape(shp[:axis] + (POOL_PAD,) + shp[axis + 1:])


def kernel(x, ffn1_w13, ffn1_w2, ln1_g, ln1_b, w_in, gate_b, pool_w, pool_b, pool_scale, conv_w, conv_b, dt_bias, a_log, d_skip, ssd_norm, rel_bias, p_pool, p_ssd, p_attn, w_out, ln2_g, ln2_b, ffn2_w13, ffn2_w2, ln3_g, ln3_b):
    sec = [w_in[:, :, IN_OFFS[i]:IN_OFFS[i + 1]] for i in range(len(IN_SIZES))]
    w_u, w_z, w_xbc, w_dt, w_q, w_k, w_v, w_g = sec
    w_pool = _pad_groups(w_u, 2).astype(BF16)
    w_z = w_z.astype(BF16)
    w_xbc = w_xbc.astype(BF16)
    w_dt = jnp.pad(w_dt, ((0, 0), (0, 0), (0, DT_PAD - SSD_HEADS))).astype(BF16)
    w_qkv = jnp.concatenate(
        [s[:, :, gi * ATTN_OUT:(gi + 1) * ATTN_OUT] for gi in range(len(ATTN_CONFIGS)) for s in (w_q, w_k, w_v)],
        axis=2).astype(BF16)
    w_g = w_g.astype(BF16)

    pw = jnp.pad(pool_w, ((0, 0), (0, 0), (0, POOL_GPAD - POOL_GDIM), (0, POOL_GPAD - POOL_GDIM))).astype(BF16)
    pb = jnp.pad(pool_b, ((0, 0), (0, 0), (0, POOL_GPAD - POOL_GDIM)))
    ps = jnp.pad(pool_scale.reshape(DEPTH, POOL_GROUPS, POOL_GDIM), ((0, 0), (0, 0), (0, POOL_GPAD - POOL_GDIM)))
    p_pool_b = p_pool.astype(BF16)
    p_ssd_b = p_ssd.astype(BF16)
    p_attn_b = p_attn.astype(BF16)
    w_out_b = w_out.astype(BF16)

    dtb = jnp.pad(dt_bias, ((0, 0), (0, DT_PAD - SSD_HEADS)))[:, None, :]
    alog = jnp.pad(a_log, ((0, 0), (0, DT_PAD - SSD_HEADS)))[:, None, :]
    dsk = jnp.repeat(d_skip, SSD_HEADDIM, axis=1)[:, None, :]
    cb = conv_b[:, None, :]
    nw = ssd_norm[:, None, :]
    bias = _attn_bias(rel_bias)

    f1_w13, f1_w2 = ffn1_w13.astype(BF16), ffn1_w2.astype(BF16)
    f2_w13, f2_w2 = ffn2_w13.astype(BF16), ffn2_w2.astype(BF16)
    vec = lambda a: a[:, None, :]

    h = x.reshape(N_TOK, D_MODEL)
    for i in range(DEPTH):
        h, hb = _ffn(h, f1_w13, f1_w2, vec(ln1_g), vec(ln1_b), i, True)
        ya = _pool(hb, w_pool, pw, pb, ps, i)
        yb = _ssd(hb, w_z, w_xbc, w_dt, conv_w, cb, dtb, alog, dsk, nw, i)
        yc = _attn(hb, w_qkv, bias, i)
        h = _merge(h, ya, yb, yc, w_g, gate_b, p_pool_b, p_ssd_b, p_attn_b, w_out_b, vec(ln2_g), vec(ln2_b), i)
        (h,) = _ffn(h, f2_w13, f2_w2, vec(ln3_g), vec(ln3_b), i, False)
    return h.reshape(BATCH, SEQ, D_MODEL)
```
